```python
import jax, jax.numpy as jnp
from jax import lax
import numpy as np

D_MODEL = 2048
BATCH = 1
SEQ = 8192
DEPTH = 1
DEC_BATCH = 4
DEC_SEQ = 4096
PAST_LEN = 128

RET_HEADS = 8
RET_HEAD_DIM = 128
RET_WIDTH = RET_HEADS * RET_HEAD_DIM
RET_CHUNK = 128
MLA_HEADS = 8
MLA_NOPE_DIM = 128
MLA_ROPE_DIM = 64
MLA_V_DIM = 128
MLA_Q_RANK = 512
MLA_KV_RANK = 512
MLA_WIDTH = MLA_HEADS * MLA_V_DIM
MIX_WIDTH = RET_WIDTH + MLA_WIDTH
FFN_DIM = 4 * D_MODEL
ROPE_BASE = 10000.0
Q_BLOCK = 128
NORM_EPS = 1e-6
IN_SPLITS = (RET_WIDTH, RET_WIDTH, RET_WIDTH, RET_WIDTH, MLA_Q_RANK, MLA_KV_RANK, MLA_ROPE_DIM)
IN_COLS = RET_WIDTH * 4 + MLA_Q_RANK + MLA_KV_RANK + MLA_ROPE_DIM

kernel_name = "hybrid_retention_mla_encoder"


def _rmsnorm(x, g):
    xf = x.astype(jnp.float32)
    y = xf * lax.rsqrt(jnp.mean(xf * xf, axis=-1, keepdims=True) + NORM_EPS)
    return (y * g.astype(jnp.float32)).astype(x.dtype)


def _head_rms(x):
    return x * lax.rsqrt(jnp.mean(x * x, axis=-1, keepdims=True) + NORM_EPS)


def _rope_tables(S, d):
    inv = ROPE_BASE ** (-jnp.arange(0, d, 2, dtype=jnp.float32) / d)
    ang = jnp.arange(S, dtype=jnp.float32)[:, None] * inv[None, :]
    return jnp.cos(ang), jnp.sin(ang)


def _apply_rope(x, cos, sin):
    d = x.shape[-1]
    xf = x.astype(jnp.float32)
    x1, x2 = xf[..., : d // 2], xf[..., d // 2:]
    c = cos[None, :, None, :]
    s = sin[None, :, None, :]
    return jnp.concatenate([x1 * c - x2 * s, x1 * s + x2 * c], axis=-1).astype(x.dtype)


def _retention_direction(q, k, v, log_g, strict):
    B, H, S, dk = q.shape
    dv = v.shape[-1]
    C = RET_CHUNK
    n = S // C
    qc = q.reshape(B, H, n, C, dk)
    kc = k.reshape(B, H, n, C, dk)
    vc = v.reshape(B, H, n, C, dv)
    idx = jnp.arange(C, dtype=jnp.float32)
    diff = idx[:, None] - idx[None, :]
    mask = (diff > 0) if strict else (diff >= 0)
    decay_intra = jnp.where(mask[None], jnp.exp(log_g[:, None, None] * jnp.maximum(diff, 0.0)[None]), 0.0)
    scores = jnp.einsum("bhnid,bhnjd->bhnij", qc, kc) * decay_intra[None, :, None]
    out_intra = jnp.einsum("bhnij,bhnje->bhnie", scores, vc)
    w_k = jnp.exp(log_g[:, None] * (C - 1.0 - idx)[None, :])
    chunk_kv = jnp.einsum("bhnjd,hj,bhnje->bhnde", kc, w_k, vc)
    chunk_decay = jnp.exp(log_g * C)[None, :, None, None]

    def step(state, kv_n):
        return state * chunk_decay + kv_n, state

    _, states = lax.scan(step, jnp.zeros((B, H, dk, dv), jnp.float32), jnp.moveaxis(chunk_kv, 2, 0))
    states = jnp.moveaxis(states, 0, 2)
    w_q = jnp.exp(log_g[:, None] * (idx + 1.0)[None, :])
    out_cross = jnp.einsum("bhnid,hi,bhnde->bhnie", qc, w_q, states)
    return (out_intra + out_cross).reshape(B, H, S, dv)


def _retention_mixer(q, k, v, g, logit_fwd, logit_bwd, cos, sin):
    B, S, _ = q.shape
    q = _apply_rope(q.reshape(B, S, RET_HEADS, RET_HEAD_DIM), cos, sin)
    k = _apply_rope(k.reshape(B, S, RET_HEADS, RET_HEAD_DIM), cos, sin)
    qf = jnp.transpose(q.astype(jnp.float32), (0, 2, 1, 3))
    kf = jnp.transpose(k.astype(jnp.float32), (0, 2, 1, 3)) * (RET_HEAD_DIM ** -0.5)
    vf = jnp.transpose(v.reshape(B, S, RET_HEADS, RET_HEAD_DIM).astype(jnp.float32), (0, 2, 1, 3))
    lg_f = jax.nn.log_sigmoid(logit_fwd.astype(jnp.float32))
    lg_b = jax.nn.log_sigmoid(logit_bwd.astype(jnp.float32))
    o_f = _retention_direction(qf, kf, vf, lg_f, False)
    o_b = jnp.flip(_retention_direction(jnp.flip(qf, 2), jnp.flip(kf, 2), jnp.flip(vf, 2), lg_b, True), 2)
    o = _head_rms(jnp.transpose(o_f + o_b, (0, 2, 1, 3)))
    gate = jax.nn.silu(g.reshape(B, S, RET_HEADS, RET_HEAD_DIM).astype(jnp.float32))
    return (gate * o).reshape(B, S, RET_WIDTH).astype(q.dtype)


def _mla_mixer(c_q, c_kv, k_rope, q_norm, w_q_up, kv_norm, w_kv_up, cos, sin):
    B, S, _ = c_q.shape
    q = (_rmsnorm(c_q, q_norm) @ w_q_up).reshape(B, S, MLA_HEADS, MLA_NOPE_DIM + MLA_ROPE_DIM)
    q_nope = q[..., :MLA_NOPE_DIM]
    q_rope = _apply_rope(q[..., MLA_NOPE_DIM:], cos, sin)
    kv = (_rmsnorm(c_kv, kv_norm) @ w_kv_up).reshape(B, S, MLA_HEADS, MLA_NOPE_DIM + MLA_V_DIM)
    k_nope = kv[..., :MLA_NOPE_DIM]
    v = kv[..., MLA_NOPE_DIM:]
    k_r = _apply_rope(k_rope[:, :, None, :], cos, sin)[:, :, 0, :]
    scale = (MLA_NOPE_DIM + MLA_ROPE_DIM) ** -0.5
    nb = S // Q_BLOCK
    qn_blocks = jnp.transpose(q_nope.reshape(B, nb, Q_BLOCK, MLA_HEADS, MLA_NOPE_DIM), (1, 0, 3, 2, 4))
    qr_blocks = jnp.transpose(q_rope.reshape(B, nb, Q_BLOCK, MLA_HEADS, MLA_ROPE_DIM), (1, 0, 3, 2, 4))

    def attend(blk):
        qn_b, qr_b = blk
        s = jnp.einsum("bhqd,bkhd->bhqk", qn_b, k_nope) + jnp.einsum("bhqr,bkr->bhqk", qr_b, k_r)
        p = jax.nn.softmax(s.astype(jnp.float32) * scale, axis=-1)
        return jnp.einsum("bhqk,bkhd->bqhd", p.astype(v.dtype), v)

    o = lax.map(attend, (qn_blocks, qr_blocks))
    return jnp.transpose(o, (1, 0, 2, 3, 4)).reshape(B, S, MLA_WIDTH)


def _layer(x, norm_mix_pre, w_in, ret_decay_fwd, ret_decay_bwd, mla_q_norm, w_q_up,
           mla_kv_norm, w_kv_up, w_out, norm_mix_post, norm_ffn_pre, w_ffn_up,
           w_ffn_down, norm_ffn_post):
    S = x.shape[1]
    h = _rmsnorm(x, norm_mix_pre)
    proj = h @ w_in
    offs = np.cumsum(IN_SPLITS)[:-1].tolist()
    r_q, r_k, r_v, r_g, m_cq, m_ckv, m_kr = jnp.split(proj, offs, axis=-1)
    cos_r, sin_r = _rope_tables(S, RET_HEAD_DIM)
    cos_m, sin_m = _rope_tables(S, MLA_ROPE_DIM)
    o_ret = _retention_mixer(r_q, r_k, r_v, r_g, ret_decay_fwd, ret_decay_bwd, cos_r, sin_r)
    o_mla = _mla_mixer(m_cq, m_ckv, m_kr, mla_q_norm, w_q_up, mla_kv_norm, w_kv_up, cos_m, sin_m)
    mix = jnp.concatenate([o_ret.astype(x.dtype), o_mla.astype(x.dtype)], axis=-1) @ w_out
    x = x + _rmsnorm(mix, norm_mix_post)
    h = _rmsnorm(x, norm_ffn_pre)
    u = jnp.square(jax.nn.relu(h @ w_ffn_up)) @ w_ffn_down
    return x + _rmsnorm(u, norm_ffn_post)


def setup_inputs(seed: int = 0) -> dict:
    key = jax.random.key(seed)
    ks = jax.random.split(key, 16)
    f32 = jnp.float32

    def nrm(k, shape, fan_in):
        return jax.random.normal(k, shape, f32) * (fan_in ** -0.5)

    def gain(k, width):
        return 1.0 + 0.05 * jax.random.normal(k, (DEPTH, width), f32)

    base_logit = jnp.log(jnp.exp2(5.0 + jnp.arange(RET_HEADS, dtype=f32)) - 1.0)
    return {
        "x_prompt": jax.random.normal(ks[0], (BATCH, SEQ, D_MODEL), f32),
        "x_sample": jax.random.normal(ks[1], (DEC_BATCH, DEC_SEQ, D_MODEL), f32),
        "norm_mix_pre": gain(ks[2], D_MODEL),
        "w_in": nrm(ks[3], (DEPTH, D_MODEL, IN_COLS), D_MODEL),
        "ret_decay_fwd": base_logit[None, :] + 0.1 * jax.random.normal(ks[4], (DEPTH, RET_HEADS), f32),
        "ret_decay_bwd": base_logit[None, :] + 0.1 * jax.random.normal(ks[5], (DEPTH, RET_HEADS), f32),
        "mla_q_norm": gain(ks[6], MLA_Q_RANK),
        "w_q_up": nrm(ks[7], (DEPTH, MLA_Q_RANK, MLA_HEADS * (MLA_NOPE_DIM + MLA_ROPE_DIM)), MLA_Q_RANK),
        "mla_kv_norm": gain(ks[8], MLA_KV_RANK),
        "w_kv_up": nrm(ks[9], (DEPTH, MLA_KV_RANK, MLA_HEADS * (MLA_NOPE_DIM + MLA_V_DIM)), MLA_KV_RANK),
        "w_out": nrm(ks[10], (DEPTH, MIX_WIDTH, D_MODEL), MIX_WIDTH),
        "norm_mix_post": gain(ks[11], D_MODEL),
        "norm_ffn_pre": gain(ks[12], D_MODEL),
        "w_ffn_up": nrm(ks[13], (DEPTH, D_MODEL, FFN_DIM), D_MODEL),
        "w_ffn_down": nrm(ks[14], (DEPTH, FFN_DIM, D_MODEL), FFN_DIM),
        "norm_ffn_post": gain(ks[15], D_MODEL),
    }


def reference(x_prompt, x_sample, norm_mix_pre, w_in, ret_decay_fwd, ret_decay_bwd,
              mla_q_norm, w_q_up, mla_kv_norm, w_kv_up, w_out, norm_mix_post,
              norm_ffn_pre, w_ffn_up, w_ffn_down, norm_ffn_post):
    y_prompt = x_prompt
    y_sample = x_sample
    for l in range(DEPTH):
        params = (norm_mix_pre[l], w_in[l], ret_decay_fwd[l], ret_decay_bwd[l], mla_q_norm[l],
                  w_q_up[l], mla_kv_norm[l], w_kv_up[l], w_out[l], norm_mix_post[l],
                  norm_ffn_pre[l], w_ffn_up[l], w_ffn_down[l], norm_ffn_post[l])
        y_prompt = _layer(y_prompt, *params)
        y_sample = _layer(y_sample, *params)
    return (y_prompt, y_sample)
```

```python
import functools
import math

import jax
import jax.numpy as jnp
from jax import lax
from jax.experimental import pallas as pl
from jax.experimental.pallas import tpu as pltpu

D_MODEL = 2048
RET_HEADS = 8
RET_HEAD_DIM = 128
RET_WIDTH = RET_HEADS * RET_HEAD_DIM
RET_CHUNK = 128
MLA_HEADS = 8
MLA_NOPE_DIM = 128
MLA_ROPE_DIM = 64
MLA_V_DIM = 128
MLA_Q_RANK = 512
MLA_KV_RANK = 512
MLA_WIDTH = MLA_HEADS * MLA_V_DIM
FFN_DIM = 4 * D_MODEL
ROPE_BASE = 10000.0
NORM_EPS = 1e-6

LANES = 128
MLA_QK_DIM = 2 * LANES
IN_COLS_PADDED = 4 * RET_WIDTH + MLA_Q_RANK + MLA_KV_RANK + LANES
VMEM_LIMIT_BYTES = 56 * 1024 * 1024

F32 = jnp.float32
BF16 = jnp.bfloat16

TM_IN = 256
TM_OUT = 512
TM_FFN = 512
TF_FFN = 512
TQ_ATT = 512
TK_ATT = 512


def _const_spec(shape):
    return pl.BlockSpec(shape, lambda *_: (0,) * len(shape), pipeline_mode=pl.Buffered(1))


def _rms(x, gain):
    return x * lax.rsqrt(jnp.mean(x * x, axis=-1, keepdims=True) + NORM_EPS) * gain


def _rope_slab(x, cos, sin_signed):
    return x * cos + pltpu.roll(x, LANES // 2, 1) * sin_signed


def _in_proj_kernel(x_ref, g_ref, w_ref, qg_ref, kvg_ref, wq_ref, wkv_ref, cr_ref, sr_ref, cm_ref, sm_ref,
                    rq_ref, rkt_ref, rv_ref, gate_ref, qcat_ref, kcat_ref, vt_ref):
    h = _rms(x_ref[...], g_ref[...]).astype(BF16)

    def proj(lo, hi):
        return jnp.dot(h, w_ref[:, lo:hi], preferred_element_type=F32)

    cr, sr = cr_ref[...], sr_ref[...]
    cm, sm = cm_ref[...], sm_ref[...]

    q = proj(0, RET_WIDTH)
    for hh in range(RET_HEADS):
        sl = slice(hh * LANES, (hh + 1) * LANES)
        rq_ref[:, sl] = _rope_slab(q[:, sl], cr, sr).astype(BF16)

    k = proj(RET_WIDTH, 2 * RET_WIDTH)
    k_scale = RET_HEAD_DIM ** -0.5
    for hh in range(RET_HEADS):
        sl = slice(hh * LANES, (hh + 1) * LANES)
        rkt_ref[0, sl, :] = (_rope_slab(k[:, sl], cr, sr) * k_scale).T.astype(BF16)

    rv_ref[...] = proj(2 * RET_WIDTH, 3 * RET_WIDTH).astype(BF16)

    g = proj(3 * RET_WIDTH, 4 * RET_WIDTH)
    gate_ref[...] = (g / (1.0 + jnp.exp(-g))).astype(BF16)

    off = 4 * RET_WIDTH
    cq = _rms(proj(off, off + MLA_Q_RANK), qg_ref[...]).astype(BF16)
    qf = jnp.dot(cq, wq_ref[...], preferred_element_type=F32) * ((MLA_NOPE_DIM + MLA_ROPE_DIM) ** -0.5)
    lane = lax.broadcasted_iota(jnp.int32, (1, LANES), 1)
    even_lanes = (lane // (MLA_ROPE_DIM // 2)) % 2 == 0
    nope_w = MLA_HEADS * MLA_NOPE_DIM
    for j in range(MLA_HEADS // 2):
        slab = _rope_slab(qf[:, nope_w + j * LANES: nope_w + (j + 1) * LANES], cm, sm)
        for par in range(2):
            hh = 2 * j + par
            base = hh * MLA_QK_DIM
            qcat_ref[:, base: base + LANES] = qf[:, hh * LANES: (hh + 1) * LANES].astype(BF16)
            keep = even_lanes if par == 0 else jnp.logical_not(even_lanes)
            qcat_ref[:, base + LANES: base + 2 * LANES] = jnp.where(keep, slab, 0.0).astype(BF16)

    off += MLA_Q_RANK
    ckv = _rms(proj(off, off + MLA_KV_RANK), kvg_ref[...]).astype(BF16)
    kvf = jnp.dot(ckv, wkv_ref[...], preferred_element_type=F32)
    off += MLA_KV_RANK
    kext = _rope_slab(proj(off, off + LANES), cm, sm).astype(BF16)
    for hh in range(MLA_HEADS):
        base = hh * MLA_QK_DIM
        kcat_ref[:, base: base + LANES] = kvf[:, hh * LANES: (hh + 1) * LANES].astype(BF16)
        kcat_ref[:, base + LANES: base + 2 * LANES] = kext
    for hh in range(MLA_HEADS):
        sl = slice(nope_w + hh * LANES, nope_w + (hh + 1) * LANES)
        vt_ref[0, hh * LANES:(hh + 1) * LANES, :] = kvf[:, sl].T.astype(BF16)


def _in_proj(x2, B, S, p):
    T = B * S
    tm = TM_IN
    ns = S // tm
    row = lambda i: (i, 0)
    pos = lambda i: (i % ns, 0)
    tpose = lambda i: (i // ns, 0, i % ns)
    bf = lambda n: jax.ShapeDtypeStruct((T, n), BF16)
    bft = jax.ShapeDtypeStruct((B, RET_WIDTH, S), BF16)
    return pl.pallas_call(
        _in_proj_kernel,
        grid=(T // tm,),
        in_specs=[
            pl.BlockSpec((tm, D_MODEL), row),
            _const_spec((1, D_MODEL)),
            _const_spec((D_MODEL, IN_COLS_PADDED)),
            _const_spec((1, MLA_Q_RANK)),
            _const_spec((1, MLA_KV_RANK)),
            _const_spec((MLA_Q_RANK, MLA_HEADS * (MLA_NOPE_DIM + MLA_ROPE_DIM))),
            _const_spec((MLA_KV_RANK, MLA_HEADS * (MLA_NOPE_DIM + MLA_V_DIM))),
            pl.BlockSpec((tm, LANES), pos),
            pl.BlockSpec((tm, LANES), pos),
            pl.BlockSpec((tm, LANES), pos),
            pl.BlockSpec((tm, LANES), pos),
        ],
        out_specs=[
            pl.BlockSpec((tm, RET_WIDTH), row),
            pl.BlockSpec((1, RET_WIDTH, tm), tpose),
            pl.BlockSpec((tm, RET_WIDTH), row),
            pl.BlockSpec((tm, RET_WIDTH), row),
            pl.BlockSpec((tm, MLA_HEADS * MLA_QK_DIM), row),
            pl.BlockSpec((tm, MLA_HEADS * MLA_QK_DIM), row),
            pl.BlockSpec((1, MLA_WIDTH, tm), tpose),
        ],
        out_shape=[bf(RET_WIDTH), bft, bf(RET_WIDTH), bf(RET_WIDTH),
                   bf(MLA_HEADS * MLA_QK_DIM), bf(MLA_HEADS * MLA_QK_DIM), bft],
        compiler_params=pltpu.CompilerParams(
            dimension_semantics=("parallel",), vmem_limit_bytes=VMEM_LIMIT_BYTES),
        name="in_proj",
    )(x2, p["g_mix_pre"], p["w_in"], p["g_q"], p["g_kv"], p["w_q_up"], p["w_kv_up"],
      p["cos_r"], p["sin_r"], p["cos_m"], p["sin_m"])


def _log_sigmoid(x):
    return jnp.minimum(x, 0.0) - jnp.log(1.0 + jnp.exp(-jnp.abs(x)))


def _retention_kernel(q_ref, kt_ref, v_ref, g_ref, lf_ref, lb_ref, o_ref, st_ref, *, n_chunks):
    C = RET_CHUNK
    lf = _log_sigmoid(lf_ref[0])
    lb = _log_sigmoid(lb_ref[0])
    row = lax.broadcasted_iota(jnp.int32, (C, C), 0).astype(F32)
    col = lax.broadcasted_iota(jnp.int32, (C, C), 1).astype(F32)
    diff = row - col
    decay = jnp.exp(jnp.where(diff >= 0, lf * diff, -lb * diff))
    wkf_t = jnp.exp(lf * (C - 1.0 - col))
    wkb_t = jnp.exp(lb * col)
    wqf = jnp.exp(lf * (row + 1.0))
    wqb = jnp.exp(lb * (C - row))
    df = jnp.exp(lf * C)
    db = jnp.exp(lb * C)

    def chunk_kv(c, w_t):
        off = pl.multiple_of(c * C, C)
        kt = kt_ref[0, :, pl.ds(off, C)].astype(F32)
        return jnp.dot((kt * w_t).astype(BF16), v_ref[0, pl.ds(off, C), :], preferred_element_type=F32)

    def fwd_state(c, state):
        st_ref[c, 0:C, :] = state.astype(BF16)
        return state * df + chunk_kv(c, wkf_t)

    def bwd_state(t, state):
        c = n_chunks - 1 - t
        st_ref[c, C:2 * C, :] = state.astype(BF16)
        return state * db + chunk_kv(c, wkb_t)

    zero = jnp.zeros((C, C), F32)
    lax.fori_loop(0, n_chunks, fwd_state, zero)
    lax.fori_loop(0, n_chunks, bwd_state, zero)

    def out_chunk(c, carry):
        off = pl.multiple_of(c * C, C)
        q = q_ref[0, pl.ds(off, C), :]
        v = v_ref[0, pl.ds(off, C), :]
        s = jnp.dot(q, kt_ref[0, :, pl.ds(off, C)], preferred_element_type=F32)
        o = jnp.dot((s * decay).astype(BF16), v, preferred_element_type=F32)
        qf = q.astype(F32)
        qq = jnp.concatenate([(qf * wqf).astype(BF16), (qf * wqb).astype(BF16)], axis=1)
        o = o + jnp.dot(qq, st_ref[c], preferred_element_type=F32)
        o = o * lax.rsqrt(jnp.mean(o * o, axis=-1, keepdims=True) + NORM_EPS)
        o_ref[0, pl.ds(off, C), :] = (o * g_ref[0, pl.ds(off, C), :].astype(F32)).astype(BF16)
        return carry

    lax.fori_loop(0, n_chunks, out_chunk, 0)


def _retention(rq, rkt, rv, gate, lf, lb, B, S):
    n = S // RET_CHUNK
    tok = pl.BlockSpec((1, S, LANES), lambda b, h: (b, 0, h))
    tok_t = pl.BlockSpec((1, LANES, S), lambda b, h: (b, h, 0))
    dec = pl.BlockSpec((1, 1, LANES), lambda b, h: (h, 0, 0))
    return pl.pallas_call(
        functools.partial(_retention_kernel, n_chunks=n),
        grid=(B, RET_HEADS),
        in_specs=[tok, tok_t, tok, tok, dec, dec],
        out_specs=tok,
        out_shape=jax.ShapeDtypeStruct((B, S, RET_WIDTH), BF16),
        scratch_shapes=[pltpu.VMEM((n, 2 * RET_CHUNK, RET_HEAD_DIM), BF16)],
        compiler_params=pltpu.CompilerParams(
            dimension_semantics=("parallel", "parallel"), vmem_limit_bytes=VMEM_LIMIT_BYTES),
        name="retention",
    )(rq.reshape(B, S, RET_WIDTH), rkt, rv.reshape(B, S, RET_WIDTH), gate.reshape(B, S, RET_WIDTH), lf, lb)


def _attention_kernel(q_ref, k_ref, vt_ref, o_ref, acc_ref, *, n_kv, tk):
    q = q_ref[0]
    tq = q.shape[0]

    def kv_step(j, carry):
        m, l = carry
        off = pl.multiple_of(j * tk, tk)
        k = k_ref[0, pl.ds(off, tk), :]
        s = lax.dot_general(k, q, (((1,), (1,)), ((), ())), preferred_element_type=F32)
        m_new = jnp.maximum(m, jnp.max(s, axis=0, keepdims=True))
        alpha = jnp.exp(m - m_new)
        p = jnp.exp(s - m_new)
        l_new = alpha * l + jnp.sum(p, axis=0, keepdims=True)
        pv = jnp.dot(vt_ref[0, :, pl.ds(off, tk)], p.astype(BF16), preferred_element_type=F32)
        acc_ref[...] = alpha * acc_ref[...] + pv
        return m_new, l_new

    acc_ref[...] = jnp.zeros_like(acc_ref)
    m0 = jnp.full((1, tq), -1e30, F32)
    l0 = jnp.zeros((1, tq), F32)
    _, l = lax.fori_loop(0, n_kv, kv_step, (m0, l0))
    o_ref[0] = (acc_ref[...] / l).T.astype(BF16)


def _attention(qcat, kcat, vt, B, S):
    tq, tk = TQ_ATT, TK_ATT
    return pl.pallas_call(
        functools.partial(_attention_kernel, n_kv=S // tk, tk=tk),
        grid=(B, MLA_HEADS, S // tq),
        in_specs=[
            pl.BlockSpec((1, tq, MLA_QK_DIM), lambda b, h, i: (b, i, h)),
            pl.BlockSpec((1, S, MLA_QK_DIM), lambda b, h, i: (b, 0, h)),
            pl.BlockSpec((1, MLA_V_DIM, S), lambda b, h, i: (b, h, 0)),
        ],
        out_specs=pl.BlockSpec((1, tq, MLA_V_DIM), lambda b, h, i: (b, i, h)),
        out_shape=jax.ShapeDtypeStruct((B, S, MLA_WIDTH), BF16),
        scratch_shapes=[pltpu.VMEM((MLA_V_DIM, tq), F32)],
        compiler_params=pltpu.CompilerParams(
            dimension_semantics=("parallel", "parallel", "arbitrary"), vmem_limit_bytes=VMEM_LIMIT_BYTES),
        name="mla_attention",
    )(qcat.reshape(B, S, MLA_HEADS * MLA_QK_DIM), kcat.reshape(B, S, MLA_HEADS * MLA_QK_DIM), vt)


def _out_proj_kernel(ret_ref, mla_ref, x_ref, w_ref, g_ref, o_ref):
    mix = jnp.dot(ret_ref[...], w_ref[0:RET_WIDTH, :], preferred_element_type=F32)
    mix = mix + jnp.dot(mla_ref[...], w_ref[RET_WIDTH:, :], preferred_element_type=F32)
    o_ref[...] = x_ref[...] + _rms(mix, g_ref[...])


def _out_proj(o_ret, o_mla, x2, p):
    T = x2.shape[0]
    tm = TM_OUT
    row = lambda i: (i, 0)
    return pl.pallas_call(
        _out_proj_kernel,
        grid=(T // tm,),
        in_specs=[
            pl.BlockSpec((tm, RET_WIDTH), row),
            pl.BlockSpec((tm, MLA_WIDTH), row),
            pl.BlockSpec((tm, D_MODEL), row),
            _const_spec((RET_WIDTH + MLA_WIDTH, D_MODEL)),
            _const_spec((1, D_MODEL)),
        ],
        out_specs=pl.BlockSpec((tm, D_MODEL), row),
        out_shape=jax.ShapeDtypeStruct((T, D_MODEL), F32),
        compiler_params=pltpu.CompilerParams(
            dimension_semantics=("parallel",), vmem_limit_bytes=VMEM_LIMIT_BYTES),
        name="out_proj",
    )(o_ret.reshape(T, RET_WIDTH), o_mla.reshape(T, MLA_WIDTH), x2, p["w_out"], p["g_mix_post"])


def _ffn_kernel(x_ref, gpre_ref, wup_ref, wdn_ref, gpost_ref, o_ref, h_ref, acc_ref):
    j = pl.program_id(1)

    @pl.when(j == 0)
    def _():
        h_ref[...] = _rms(x_ref[...], gpre_ref[...]).astype(BF16)
        acc_ref[...] = jnp.zeros_like(acc_ref)

    a = jnp.maximum(jnp.dot(h_ref[...], wup_ref[...], preferred_element_type=F32), 0.0)
    acc_ref[...] += jnp.dot((a * a).astype(BF16), wdn_ref[...], preferred_element_type=F32)

    @pl.when(j == pl.num_programs(1) - 1)
    def _():
        o_ref[...] = x_ref[...] + _rms(acc_ref[...], gpost_ref[...])


def _ffn(x1, p):
    T = x1.shape[0]
    tm, tf = TM_FFN, TF_FFN
    return pl.pallas_call(
        _ffn_kernel,
        grid=(T // tm, FFN_DIM // tf),
        in_specs=[
            pl.BlockSpec((tm, D_MODEL), lambda i, j: (i, 0)),
            _const_spec((1, D_MODEL)),
            pl.BlockSpec((D_MODEL, tf), lambda i, j: (0, j)),
            pl.BlockSpec((tf, D_MODEL), lambda i, j: (j, 0)),
            _const_spec((1, D_MODEL)),
        ],
        out_specs=pl.BlockSpec((tm, D_MODEL), lambda i, j: (i, 0)),
        out_shape=jax.ShapeDtypeStruct((T, D_MODEL), F32),
        scratch_shapes=[pltpu.VMEM((tm, D_MODEL), BF16), pltpu.VMEM((tm, D_MODEL), F32)],
        compiler_params=pltpu.CompilerParams(
            dimension_semantics=("parallel", "arbitrary"), vmem_limit_bytes=VMEM_LIMIT_BYTES),
        name="ffn",
    )(x1, p["g_ffn_pre"], p["w_ffn_up"], p["w_ffn_down"], p["g_ffn_post"])


def _rope_tables(S):
    pos = jnp.arange(S, dtype=F32)[:, None]

    def ang(d):
        inv = ROPE_BASE ** (-jnp.arange(0, d, 2, dtype=F32) / d)
        return pos * inv[None, :]

    a_r = ang(RET_HEAD_DIM)
    a_m = ang(MLA_ROPE_DIM)
    cos_r = jnp.concatenate([jnp.cos(a_r)] * 2, axis=1)
    sin_r = jnp.concatenate([-jnp.sin(a_r), jnp.sin(a_r)], axis=1)
    cos_m = jnp.concatenate([jnp.cos(a_m)] * 4, axis=1)
    sin_m = jnp.concatenate([-jnp.sin(a_m)] * 2 + [jnp.sin(a_m)] * 2, axis=1)
    return cos_r, sin_r, cos_m, sin_m


def _layout_params(norm_mix_pre, w_in, ret_decay_fwd, ret_decay_bwd, mla_q_norm, w_q_up, mla_kv_norm, w_kv_up,
                   w_out, norm_mix_post, norm_ffn_pre, w_ffn_up, w_ffn_down, norm_ffn_post):
    half = MLA_ROPE_DIM // 2
    main = 4 * RET_WIDTH + MLA_Q_RANK + MLA_KV_RANK
    kr1, kr2 = w_in[:, main: main + half], w_in[:, main + half:]
    w_in_p = jnp.concatenate([w_in[:, :main], kr1, kr1, kr2, kr2], axis=1).astype(BF16)

    per_head = MLA_NOPE_DIM + MLA_ROPE_DIM
    wq = w_q_up.reshape(MLA_Q_RANK, MLA_HEADS, per_head)
    nope = wq[:, :, :MLA_NOPE_DIM].reshape(MLA_Q_RANK, MLA_HEADS * MLA_NOPE_DIM)
    x1 = wq[:, :, MLA_NOPE_DIM: MLA_NOPE_DIM + half].reshape(MLA_Q_RANK, MLA_HEADS // 2, 2 * half)
    x2 = wq[:, :, MLA_NOPE_DIM + half:].reshape(MLA_Q_RANK, MLA_HEADS // 2, 2 * half)
    slabs = jnp.concatenate([x1, x2], axis=2).reshape(MLA_Q_RANK, MLA_HEADS * MLA_ROPE_DIM)
    w_q_p = jnp.concatenate([nope, slabs], axis=1).astype(BF16)

    wkv = w_kv_up.reshape(MLA_KV_RANK, MLA_HEADS, MLA_NOPE_DIM + MLA_V_DIM)
    w_kv_p = jnp.concatenate([
        wkv[:, :, :MLA_NOPE_DIM].reshape(MLA_KV_RANK, MLA_HEADS * MLA_NOPE_DIM),
        wkv[:, :, MLA_NOPE_DIM:].reshape(MLA_KV_RANK, MLA_HEADS * MLA_V_DIM)], axis=1).astype(BF16)

    bcast = lambda v: jnp.broadcast_to(v.astype(F32)[:, None, None], (RET_HEADS, 1, LANES))
    return {
        "g_mix_pre": norm_mix_pre.reshape(1, D_MODEL), "w_in": w_in_p,
        "lf": bcast(ret_decay_fwd), "lb": bcast(ret_decay_bwd),
        "g_q": mla_q_norm.reshape(1, MLA_Q_RANK), "w_q_up": w_q_p,
        "g_kv": mla_kv_norm.reshape(1, MLA_KV_RANK), "w_kv_up": w_kv_p,
        "w_out": w_out.astype(BF16), "g_mix_post": norm_mix_post.reshape(1, D_MODEL),
        "g_ffn_pre": norm_ffn_pre.reshape(1, D_MODEL), "w_ffn_up": w_ffn_up.astype(BF16),
        "w_ffn_down": w_ffn_down.astype(BF16), "g_ffn_post": norm_ffn_post.reshape(1, D_MODEL),
    }


def _layer(x, p):
    B, S, _ = x.shape
    x2 = x.reshape(B * S, D_MODEL)
    p = dict(p)
    p["cos_r"], p["sin_r"], p["cos_m"], p["sin_m"] = _rope_tables(S)
    rq, rkt, rv, gate, qcat, kcat, vt = _in_proj(x2, B, S, p)
    o_ret = _retention(rq, rkt, rv, gate, p["lf"], p["lb"], B, S)
    o_mla = _attention(qcat, kcat, vt, B, S)
    x1 = _out_proj(o_ret, o_mla, x2, p)
    return _ffn(x1, p).reshape(B, S, D_MODEL)


def kernel(x_prompt, x_sample, norm_mix_pre, w_in, ret_decay_fwd, ret_decay_bwd, mla_q_norm, w_q_up, mla_kv_norm,
           w_kv_up, w_out, norm_mix_post, norm_ffn_pre, w_ffn_up, w_ffn_down, norm_ffn_post):
    y_prompt, y_sample = x_prompt, x_sample
    for l in range(norm_mix_pre.shape[0]):
        p = _layout_params(norm_mix_pre[l], w_in[l], ret_decay_fwd[l], ret_decay_bwd[l], mla_q_norm[l], w_q_up[l],
                           mla_kv_norm[l], w_kv_up[l], w_out[l], norm_mix_post[l], norm_ffn_pre[l], w_ffn_up[l],
                           w_ffn_down[l], norm_ffn_post[l])
        y_prompt = _layer(y_prompt, p)
        y_sample = _layer(y_sample, p)
    return (y_prompt, y_sample)
```

```python
import functools
import math

import jax
import jax.numpy as jnp
from jax import lax
from jax.experimental import pallas as pl
from jax.experimental.pallas import tpu as pltpu

D_MODEL = 2048
RET_HEADS = 8
RET_HEAD_DIM = 128
RET_WIDTH = RET_HEADS * RET_HEAD_DIM
RET_CHUNK = 128
MLA_HEADS = 8
MLA_NOPE_DIM = 128
MLA_ROPE_DIM = 64
MLA_V_DIM = 128
MLA_Q_RANK = 512
MLA_KV_RANK = 512
MLA_WIDTH = MLA_HEADS * MLA_V_DIM
FFN_DIM = 4 * D_MODEL
ROPE_BASE = 10000.0
NORM_EPS = 1e-6

LANES = 128
MLA_QK_DIM = 2 * LANES
IN_COLS_PADDED = 4 * RET_WIDTH + MLA_Q_RANK + MLA_KV_RANK + LANES
VMEM_LIMIT_BYTES = 56 * 1024 * 1024

F32 = jnp.float32
Q_SCALE_LOG2 = (MLA_NOPE_DIM + MLA_ROPE_DIM) ** -0.5 * math.log2(math.e)
BF16 = jnp.bfloat16

TM_IN = 256
TM_OUT = 512
TM_FFN = 1024
TF_FFN = 512
TQ_ATT = 1024
TK_ATT = 1024
RET_UNROLL = 8
KV_UNROLL = 2


def _const_spec(shape):
    return pl.BlockSpec(shape, lambda *_: (0,) * len(shape), pipeline_mode=pl.Buffered(1))


def _rms(x, gain):
    return x * lax.rsqrt(jnp.mean(x * x, axis=-1, keepdims=True) + NORM_EPS) * gain


def _rope_slab(x, cos, sin_signed):
    return x * cos + pltpu.roll(x, LANES // 2, 1) * sin_signed


def _in_proj_kernel(x_ref, g_ref, w_ref, qg_ref, kvg_ref, wq_ref, wkv_ref, cr_ref, sr_ref, cm_ref, sm_ref,
                    rq_ref, rkt_ref, rv_ref, gate_ref, qcat_ref, kcat_ref, vt_ref):
    h = _rms(x_ref[...], g_ref[...]).astype(BF16)

    def proj(lo, hi):
        return jnp.dot(h, w_ref[:, lo:hi], preferred_element_type=F32)

    cr, sr = cr_ref[...], sr_ref[...]
    cm, sm = cm_ref[...], sm_ref[...]

    q = proj(0, RET_WIDTH)
    for hh in range(RET_HEADS):
        sl = slice(hh * LANES, (hh + 1) * LANES)
        rq_ref[:, sl] = _rope_slab(q[:, sl], cr, sr).astype(BF16)

    k = proj(RET_WIDTH, 2 * RET_WIDTH)
    k_scale = RET_HEAD_DIM ** -0.5
    for hh in range(RET_HEADS):
        sl = slice(hh * LANES, (hh + 1) * LANES)
        rkt_ref[0, sl, :] = (_rope_slab(k[:, sl], cr, sr) * k_scale).T.astype(BF16)

    rv_ref[...] = proj(2 * RET_WIDTH, 3 * RET_WIDTH).astype(BF16)

    g = proj(3 * RET_WIDTH, 4 * RET_WIDTH)
    gate_ref[...] = (g / (1.0 + jnp.exp(-g))).astype(BF16)

    off = 4 * RET_WIDTH
    cq = _rms(proj(off, off + MLA_Q_RANK), qg_ref[...]).astype(BF16)
    qf = jnp.dot(cq, wq_ref[...], preferred_element_type=F32) * Q_SCALE_LOG2
    lane = lax.broadcasted_iota(jnp.int32, (1, LANES), 1)
    even_lanes = (lane // (MLA_ROPE_DIM // 2)) % 2 == 0
    nope_w = MLA_HEADS * MLA_NOPE_DIM
    for j in range(MLA_HEADS // 2):
        slab = _rope_slab(qf[:, nope_w + j * LANES: nope_w + (j + 1) * LANES], cm, sm)
        for par in range(2):
            hh = 2 * j + par
            base = hh * MLA_QK_DIM
            qcat_ref[:, base: base + LANES] = qf[:, hh * LANES: (hh + 1) * LANES].astype(BF16)
            keep = even_lanes if par == 0 else jnp.logical_not(even_lanes)
            qcat_ref[:, base + LANES: base + 2 * LANES] = jnp.where(keep, slab, 0.0).astype(BF16)

    off += MLA_Q_RANK
    ckv = _rms(proj(off, off + MLA_KV_RANK), kvg_ref[...]).astype(BF16)
    kvf = jnp.dot(ckv, wkv_ref[...], preferred_element_type=F32)
    off += MLA_KV_RANK
    kext = _rope_slab(proj(off, off + LANES), cm, sm).astype(BF16)
    for hh in range(MLA_HEADS):
        base = hh * MLA_QK_DIM
        kcat_ref[:, base: base + LANES] = kvf[:, hh * LANES: (hh + 1) * LANES].astype(BF16)
        kcat_ref[:, base + LANES: base + 2 * LANES] = kext
    for hh in range(MLA_HEADS):
        sl = slice(nope_w + hh * LANES, nope_w + (hh + 1) * LANES)
        vt_ref[0, hh * LANES:(hh + 1) * LANES, :] = kvf[:, sl].T.astype(BF16)


def _in_proj(x2, B, S, p):
    T = B * S
    tm = TM_IN
    ns = S // tm
    row = lambda i: (i, 0)
    pos = lambda i: (i % ns, 0)
    tpose = lambda i: (i // ns, 0, i % ns)
    bf = lambda n: jax.ShapeDtypeStruct((T, n), BF16)
    bft = jax.ShapeDtypeStruct((B, RET_WIDTH, S), BF16)
    return pl.pallas_call(
        _in_proj_kernel,
        grid=(T // tm,),
        in_specs=[
            pl.BlockSpec((tm, D_MODEL), row),
            _const_spec((1, D_MODEL)),
            _const_spec((D_MODEL, IN_COLS_PADDED)),
            _const_spec((1, MLA_Q_RANK)),
            _const_spec((1, MLA_KV_RANK)),
            _const_spec((MLA_Q_RANK, MLA_HEADS * (MLA_NOPE_DIM + MLA_ROPE_DIM))),
            _const_spec((MLA_KV_RANK, MLA_HEADS * (MLA_NOPE_DIM + MLA_V_DIM))),
            pl.BlockSpec((tm, LANES), pos),
            pl.BlockSpec((tm, LANES), pos),
            pl.BlockSpec((tm, LANES), pos),
            pl.BlockSpec((tm, LANES), pos),
        ],
        out_specs=[
            pl.BlockSpec((tm, RET_WIDTH), row),
            pl.BlockSpec((1, RET_WIDTH, tm), tpose),
            pl.BlockSpec((tm, RET_WIDTH), row),
            pl.BlockSpec((tm, RET_WIDTH), row),
            pl.BlockSpec((tm, MLA_HEADS * MLA_QK_DIM), row),
            pl.BlockSpec((tm, MLA_HEADS * MLA_QK_DIM), row),
            pl.BlockSpec((1, MLA_WIDTH, tm), tpose),
        ],
        out_shape=[bf(RET_WIDTH), bft, bf(RET_WIDTH), bf(RET_WIDTH),
                   bf(MLA_HEADS * MLA_QK_DIM), bf(MLA_HEADS * MLA_QK_DIM), bft],
        compiler_params=pltpu.CompilerParams(
            dimension_semantics=("parallel",), vmem_limit_bytes=VMEM_LIMIT_BYTES),
        name="in_proj",
    )(x2, p["g_mix_pre"], p["w_in"], p["g_q"], p["g_kv"], p["w_q_up"], p["w_kv_up"],
      p["cos_r"], p["sin_r"], p["cos_m"], p["sin_m"])


def _log_sigmoid(x):
    return jnp.minimum(x, 0.0) - jnp.log(1.0 + jnp.exp(-jnp.abs(x)))


def _retention_kernel(q_ref, kt_ref, v_ref, g_ref, lf_ref, lb_ref, o_ref, st_ref, *, n_chunks):
    C = RET_CHUNK
    lf = _log_sigmoid(lf_ref[0])
    lb = _log_sigmoid(lb_ref[0])
    row = lax.broadcasted_iota(jnp.int32, (C, C), 0).astype(F32)
    col = lax.broadcasted_iota(jnp.int32, (C, C), 1).astype(F32)
    diff = row - col
    decay = jnp.exp(jnp.where(diff >= 0, lf * diff, -lb * diff))
    wkf_t = jnp.exp(lf * (C - 1.0 - col))
    wkb_t = jnp.exp(lb * col)
    wqf = jnp.exp(lf * (row + 1.0))
    wqb = jnp.exp(lb * (C - row))
    df = jnp.exp(lf * C)
    db = jnp.exp(lb * C)

    def chunk_kv(c, w_t):
        off = pl.multiple_of(c * C, C)
        kt = kt_ref[0, :, pl.ds(off, C)].astype(F32)
        return jnp.dot((kt * w_t).astype(BF16), v_ref[0, pl.ds(off, C), :], preferred_element_type=F32)

    def scan_states(t, states):
        sf, sb = states
        cb = n_chunks - 1 - t
        st_ref[t, 0:C, :] = sf.astype(BF16)
        st_ref[cb, C:2 * C, :] = sb.astype(BF16)
        return sf * df + chunk_kv(t, wkf_t), sb * db + chunk_kv(cb, wkb_t)

    zero = jnp.zeros((C, C), F32)
    lax.fori_loop(0, n_chunks, scan_states, (zero, zero), unroll=RET_UNROLL)

    def out_chunk(c, carry):
        off = pl.multiple_of(c * C, C)
        q = q_ref[0, pl.ds(off, C), :]
        v = v_ref[0, pl.ds(off, C), :]
        s = jnp.dot(q, kt_ref[0, :, pl.ds(off, C)], preferred_element_type=F32)
        o = jnp.dot((s * decay).astype(BF16), v, preferred_element_type=F32)
        qf = q.astype(F32)
        qq = jnp.concatenate([(qf * wqf).astype(BF16), (qf * wqb).astype(BF16)], axis=1)
        o = o + jnp.dot(qq, st_ref[c], preferred_element_type=F32)
        o = o * lax.rsqrt(jnp.mean(o * o, axis=-1, keepdims=True) + NORM_EPS)
        o_ref[0, pl.ds(off, C), :] = (o * g_ref[0, pl.ds(off, C), :].astype(F32)).astype(BF16)
        return carry

    lax.fori_loop(0, n_chunks, out_chunk, 0, unroll=RET_UNROLL)


def _retention(rq, rkt, rv, gate, lf, lb, B, S):
    n = S // RET_CHUNK
    tok = pl.BlockSpec((1, S, LANES), lambda b, h: (b, 0, h))
    tok_t = pl.BlockSpec((1, LANES, S), lambda b, h: (b, h, 0))
    dec = pl.BlockSpec((1, 1, LANES), lambda b, h: (h, 0, 0))
    return pl.pallas_call(
        functools.partial(_retention_kernel, n_chunks=n),
        grid=(B, RET_HEADS),
        in_specs=[tok, tok_t, tok, tok, dec, dec],
        out_specs=tok,
        out_shape=jax.ShapeDtypeStruct((B, S, RET_WIDTH), BF16),
        scratch_shapes=[pltpu.VMEM((n, 2 * RET_CHUNK, RET_HEAD_DIM), BF16)],
        compiler_params=pltpu.CompilerParams(
            dimension_semantics=("parallel", "parallel"), vmem_limit_bytes=VMEM_LIMIT_BYTES),
        name="retention",
    )(rq.reshape(B, S, RET_WIDTH), rkt, rv.reshape(B, S, RET_WIDTH), gate.reshape(B, S, RET_WIDTH), lf, lb)


def _attention_kernel(q_ref, k_ref, vt_ref, o_ref, acc_ref, s_ref, *, n_kv, tk):
    q = q_ref[0]
    tq = q.shape[0]

    def produce(j, slot):
        off = pl.multiple_of(j * tk, tk)
        s_ref[slot] = lax.dot_general(k_ref[0, pl.ds(off, tk), :], q, (((1,), (1,)), ((), ())),
                                      preferred_element_type=F32)

    def consume(j, slot, m, l):
        off = pl.multiple_of(j * tk, tk)
        s = s_ref[slot]
        m_new = jnp.maximum(m, jnp.max(s, axis=0, keepdims=True))
        alpha = jnp.exp2(m - m_new)
        p = jnp.exp2(s - m_new)
        l_new = alpha * l + jnp.sum(p, axis=0, keepdims=True)
        pv = jnp.dot(vt_ref[0, :, pl.ds(off, tk)], p.astype(BF16), preferred_element_type=F32)
        acc_ref[...] = alpha * acc_ref[...] + pv
        return m_new, l_new

    def group(g, carry, last):
        m, l = carry
        j0 = g * KV_UNROLL
        for i in range(KV_UNROLL):
            if not (last and i == KV_UNROLL - 1):
                produce(j0 + i + 1, (i + 1) % 2)
            m, l = consume(j0 + i, i % 2, m, l)
        return m, l

    n_groups = n_kv // KV_UNROLL
    acc_ref[...] = jnp.zeros_like(acc_ref)
    produce(0, 0)
    carry = (jnp.full((1, tq), -1e30, F32), jnp.zeros((1, tq), F32))
    if n_groups > 1:
        carry = lax.fori_loop(0, n_groups - 1, functools.partial(group, last=False), carry)
    _, l = group(n_groups - 1, carry, last=True)
    o_ref[0] = (acc_ref[...] / l).T.astype(BF16)


def _attention(qcat, kcat, vt, B, S):
    tq, tk = TQ_ATT, TK_ATT
    return pl.pallas_call(
        functools.partial(_attention_kernel, n_kv=S // tk, tk=tk),
        grid=(B, MLA_HEADS, S // tq),
        in_specs=[
            pl.BlockSpec((1, tq, MLA_QK_DIM), lambda b, h, i: (b, i, h)),
            pl.BlockSpec((1, S, MLA_QK_DIM), lambda b, h, i: (b, 0, h)),
            pl.BlockSpec((1, MLA_V_DIM, S), lambda b, h, i: (b, h, 0)),
        ],
        out_specs=pl.BlockSpec((1, tq, MLA_V_DIM), lambda b, h, i: (b, i, h)),
        out_shape=jax.ShapeDtypeStruct((B, S, MLA_WIDTH), BF16),
        scratch_shapes=[pltpu.VMEM((MLA_V_DIM, tq), F32), pltpu.VMEM((2, tk, tq), F32)],
        compiler_params=pltpu.CompilerParams(
            dimension_semantics=("parallel", "parallel", "arbitrary"), vmem_limit_bytes=VMEM_LIMIT_BYTES),
        name="mla_attention",
    )(qcat.reshape(B, S, MLA_HEADS * MLA_QK_DIM), kcat.reshape(B, S, MLA_HEADS * MLA_QK_DIM), vt)


def _out_proj_kernel(ret_ref, mla_ref, x_ref, w_ref, g_ref, o_ref):
    mix = jnp.dot(ret_ref[...], w_ref[0:RET_WIDTH, :], preferred_element_type=F32)
    mix = mix + jnp.dot(mla_ref[...], w_ref[RET_WIDTH:, :], preferred_element_type=F32)
    o_ref[...] = x_ref[...] + _rms(mix, g_ref[...])


def _out_proj(o_ret, o_mla, x2, p):
    T = x2.shape[0]
    tm = TM_OUT
    row = lambda i: (i, 0)
    return pl.pallas_call(
        _out_proj_kernel,
        grid=(T // tm,),
        in_specs=[
            pl.BlockSpec((tm, RET_WIDTH), row),
            pl.BlockSpec((tm, MLA_WIDTH), row),
            pl.BlockSpec((tm, D_MODEL), row),
            _const_spec((RET_WIDTH + MLA_WIDTH, D_MODEL)),
            _const_spec((1, D_MODEL)),
        ],
        out_specs=pl.BlockSpec((tm, D_MODEL), row),
        out_shape=jax.ShapeDtypeStruct((T, D_MODEL), F32),
        compiler_params=pltpu.CompilerParams(
            dimension_semantics=("parallel",), vmem_limit_bytes=VMEM_LIMIT_BYTES),
        name="out_proj",
    )(o_ret.reshape(T, RET_WIDTH), o_mla.reshape(T, MLA_WIDTH), x2, p["w_out"], p["g_mix_post"])


def _ffn_kernel(x_ref, gpre_ref, wup_ref, wdn_ref, gpost_ref, o_ref, h_ref):
    j = pl.program_id(1)

    @pl.when(j == 0)
    def _():
        h_ref[...] = _rms(x_ref[...], gpre_ref[...]).astype(BF16)
        o_ref[...] = jnp.zeros_like(o_ref)

    a = jnp.maximum(jnp.dot(h_ref[...], wup_ref[...], preferred_element_type=F32), 0.0)
    o_ref[...] += jnp.dot((a * a).astype(BF16), wdn_ref[...], preferred_element_type=F32)

    @pl.when(j == pl.num_programs(1) - 1)
    def _():
        o_ref[...] = x_ref[...] + _rms(o_ref[...], gpost_ref[...])


def _ffn(x1, p):
    T = x1.shape[0]
    tm, tf = TM_FFN, TF_FFN
    return pl.pallas_call(
        _ffn_kernel,
        grid=(T // tm, FFN_DIM // tf),
        in_specs=[
            pl.BlockSpec((tm, D_MODEL), lambda i, j: (i, 0)),
            _const_spec((1, D_MODEL)),
            pl.BlockSpec((D_MODEL, tf), lambda i, j: (0, j)),
            pl.BlockSpec((tf, D_MODEL), lambda i, j: (j, 0)),
            _const_spec((1, D_MODEL)),
        ],
        out_specs=pl.BlockSpec((tm, D_MODEL), lambda i, j: (i, 0)),
        out_shape=jax.ShapeDtypeStruct((T, D_MODEL), F32),
        scratch_shapes=[pltpu.VMEM((tm, D_MODEL), BF16)],
        compiler_params=pltpu.CompilerParams(
            dimension_semantics=("parallel", "arbitrary"), vmem_limit_bytes=VMEM_LIMIT_BYTES),
        name="ffn",
    )(x1, p["g_ffn_pre"], p["w_ffn_up"], p["w_ffn_down"], p["g_ffn_post"])


def _rope_tables(S):
    pos = jnp.arange(S, dtype=F32)[:, None]

    def ang(d):
        inv = ROPE_BASE ** (-jnp.arange(0, d, 2, dtype=F32) / d)
        return pos * inv[None, :]

    a_r = ang(RET_HEAD_DIM)
    a_m = ang(MLA_ROPE_DIM)
    cos_r = jnp.concatenate([jnp.cos(a_r)] * 2, axis=1)
    sin_r = jnp.concatenate([-jnp.sin(a_r), jnp.sin(a_r)], axis=1)
    cos_m = jnp.concatenate([jnp.cos(a_m)] * 4, axis=1)
    sin_m = jnp.concatenate([-jnp.sin(a_m)] * 2 + [jnp.sin(a_m)] * 2, axis=1)
    return cos_r, sin_r, cos_m, sin_m


def _layout_params(norm_mix_pre, w_in, ret_decay_fwd, ret_decay_bwd, mla_q_norm, w_q_up, mla_kv_norm, w_kv_up,
                   w_out, norm_mix_post, norm_ffn_pre, w_ffn_up, w_ffn_down, norm_ffn_post):
    half = MLA_ROPE_DIM // 2
    main = 4 * RET_WIDTH + MLA_Q_RANK + MLA_KV_RANK
    kr1, kr2 = w_in[:, main: main + half], w_in[:, main + half:]
    w_in_p = jnp.concatenate([w_in[:, :main], kr1, kr1, kr2, kr2], axis=1).astype(BF16)

    per_head = MLA_NOPE_DIM + MLA_ROPE_DIM
    wq = w_q_up.reshape(MLA_Q_RANK, MLA_HEADS, per_head)
    nope = wq[:, :, :MLA_NOPE_DIM].reshape(MLA_Q_RANK, MLA_HEADS * MLA_NOPE_DIM)
    x1 = wq[:, :, MLA_NOPE_DIM: MLA_NOPE_DIM + half].reshape(MLA_Q_RANK, MLA_HEADS // 2, 2 * half)
    x2 = wq[:, :, MLA_NOPE_DIM + half:].reshape(MLA_Q_RANK, MLA_HEADS // 2, 2 * half)
    slabs = jnp.concatenate([x1, x2], axis=2).reshape(MLA_Q_RANK, MLA_HEADS * MLA_ROPE_DIM)
    w_q_p = jnp.concatenate([nope, slabs], axis=1).astype(BF16)

    wkv = w_kv_up.reshape(MLA_KV_RANK, MLA_HEADS, MLA_NOPE_DIM + MLA_V_DIM)
    w_kv_p = jnp.concatenate([
        wkv[:, :, :MLA_NOPE_DIM].reshape(MLA_KV_RANK, MLA_HEADS * MLA_NOPE_DIM),
        wkv[:, :, MLA_NOPE_DIM:].reshape(MLA_KV_RANK, MLA_HEADS * MLA_V_DIM)], axis=1).astype(BF16)

    bcast = lambda v: jnp.broadcast_to(v.astype(F32)[:, None, None], (RET_HEADS, 1, LANES))
    return {
        "g_mix_pre": norm_mix_pre.reshape(1, D_MODEL), "w_in": w_in_p,
        "lf": bcast(ret_decay_fwd), "lb": bcast(ret_decay_bwd),
        "g_q": mla_q_norm.reshape(1, MLA_Q_RANK), "w_q_up": w_q_p,
        "g_kv": mla_kv_norm.reshape(1, MLA_KV_RANK), "w_kv_up": w_kv_p,
        "w_out": w_out.astype(BF16), "g_mix_post": norm_mix_post.reshape(1, D_MODEL),
        "g_ffn_pre": norm_ffn_pre.reshape(1, D_MODEL), "w_ffn_up": w_ffn_up.astype(BF16),
        "w_ffn_down": w_ffn_down.astype(BF16), "g_ffn_post": norm_ffn_post.reshape(1, D_MODEL),
    }


def _layer(x, p):
    B, S, _ = x.shape
    x2 = x.reshape(B * S, D_MODEL)
    p = dict(p)
    p["cos_r"], p["sin_r"], p["cos_m"], p["sin_m"] = _rope_tables(S)
    rq, rkt, rv, gate, qcat, kcat, vt = _in_proj(x2, B, S, p)
    o_ret = _retention(rq, rkt, rv, gate, p["lf"], p["lb"], B, S)
    o_mla = _attention(qcat, kcat, vt, B, S)
    x1 = _out_proj(o_ret, o_mla, x2, p)
    return _ffn(x1, p).reshape(B, S, D_MODEL)


def kernel(x_prompt, x_sample, norm_mix_pre, w_in, ret_decay_fwd, ret_decay_bwd, mla_q_norm, w_q_up, mla_kv_norm,
           w_kv_up, w_out, norm_mix_post, norm_ffn_pre, w_ffn_up, w_ffn_down, norm_ffn_post):
    y_prompt, y_sample = x_prompt, x_sample
    for l in range(norm_mix_pre.shape[0]):
        p = _layout_params(norm_mix_pre[l], w_in[l], ret_decay_fwd[l], ret_decay_bwd[l], mla_q_norm[l], w_q_up[l],
                           mla_kv_norm[l], w_kv_up[l], w_out[l], norm_mix_post[l], norm_ffn_pre[l], w_ffn_up[l],
                           w_ffn_down[l], norm_ffn_post[l])
        y_prompt = _layer(y_prompt, p)
        y_sample = _layer(y_sample, p)
    return (y_prompt, y_sample)
```

```python
import functools
import math

import jax
import jax.numpy as jnp
from jax import lax
from jax.experimental import pallas as pl
from jax.experimental.pallas import tpu as pltpu

D_MODEL = 2048
RET_HEADS = 8
RET_HEAD_DIM = 128
RET_WIDTH = RET_HEADS * RET_HEAD_DIM
RET_CHUNK = 128
MLA_HEADS = 8
MLA_NOPE_DIM = 128
MLA_ROPE_DIM = 64
MLA_V_DIM = 128
MLA_Q_RANK = 512
MLA_KV_RANK = 512
MLA_WIDTH = MLA_HEADS * MLA_V_DIM
FFN_DIM = 4 * D_MODEL
ROPE_BASE = 10000.0
NORM_EPS = 1e-6

LANES = 128
MLA_QK_DIM = 2 * LANES
IN_COLS = 4 * RET_WIDTH + MLA_Q_RANK + MLA_KV_RANK + MLA_ROPE_DIM
VMEM_LIMIT_BYTES = 56 * 1024 * 1024

F32 = jnp.float32
Q_SCALE_LOG2 = (MLA_NOPE_DIM + MLA_ROPE_DIM) ** -0.5 * math.log2(math.e)
BF16 = jnp.bfloat16

TM_IN = 256
TM_OUT = 512
TM_FFN = 1024
TF_FFN = 512
TQ_ATT = 1024
TK_ATT = 1024
RET_UNROLL = 8
KV_UNROLL = 2


def _const_spec(shape):
    return pl.BlockSpec(shape, lambda *_: (0,) * len(shape), pipeline_mode=pl.Buffered(1))


def _rms(x, gain):
    return x * lax.rsqrt(jnp.mean(x * x, axis=-1, keepdims=True) + NORM_EPS) * gain


def _rope_slab(x, cos, sin_signed):
    return x * cos + pltpu.roll(x, LANES // 2, 1) * sin_signed


def _in_proj_kernel(x_ref, g_ref, w_ref, wkr_ref, qg_ref, kvg_ref, wq_ref, wkv_ref, cr_ref, sr_ref, cm_ref, sm_ref,
                    rq_ref, rkt_ref, rv_ref, gate_ref, qcat_ref, kcat_ref, vt_ref):
    h = _rms(x_ref[...], g_ref[...]).astype(BF16)

    def proj(lo, hi):
        return jnp.dot(h, w_ref[:, lo:hi], preferred_element_type=F32)

    cr, sr = cr_ref[...], sr_ref[...]
    cm, sm = cm_ref[...], sm_ref[...]

    q = proj(0, RET_WIDTH)
    for hh in range(RET_HEADS):
        sl = slice(hh * LANES, (hh + 1) * LANES)
        rq_ref[:, sl] = _rope_slab(q[:, sl], cr, sr).astype(BF16)

    k = proj(RET_WIDTH, 2 * RET_WIDTH)
    k_scale = RET_HEAD_DIM ** -0.5
    for hh in range(RET_HEADS):
        sl = slice(hh * LANES, (hh + 1) * LANES)
        rkt_ref[0, sl, :] = (_rope_slab(k[:, sl], cr, sr) * k_scale).T.astype(BF16)

    rv_ref[...] = proj(2 * RET_WIDTH, 3 * RET_WIDTH).astype(BF16)

    g = proj(3 * RET_WIDTH, 4 * RET_WIDTH)
    gate_ref[...] = (g / (1.0 + jnp.exp(-g))).astype(BF16)

    off = 4 * RET_WIDTH
    cq = _rms(proj(off, off + MLA_Q_RANK), qg_ref[...]).astype(BF16)
    qf = jnp.dot(cq, wq_ref[...], preferred_element_type=F32) * Q_SCALE_LOG2
    lane = lax.broadcasted_iota(jnp.int32, (1, LANES), 1)
    even_lanes = (lane // (MLA_ROPE_DIM // 2)) % 2 == 0
    nope_w = MLA_HEADS * MLA_NOPE_DIM
    for j in range(MLA_HEADS // 2):
        slab = _rope_slab(qf[:, nope_w + j * LANES: nope_w + (j + 1) * LANES], cm, sm)
        for par in range(2):
            hh = 2 * j + par
            base = hh * MLA_QK_DIM
            qcat_ref[:, base: base + LANES] = qf[:, hh * LANES: (hh + 1) * LANES].astype(BF16)
            keep = even_lanes if par == 0 else jnp.logical_not(even_lanes)
            qcat_ref[:, base + LANES: base + 2 * LANES] = jnp.where(keep, slab, 0.0).astype(BF16)

    off += MLA_Q_RANK
    ckv = _rms(proj(off, off + MLA_KV_RANK), kvg_ref[...]).astype(BF16)
    kvf = jnp.dot(ckv, wkv_ref[...], preferred_element_type=F32)
    kext = _rope_slab(jnp.dot(h, wkr_ref[...], preferred_element_type=F32), cm, sm).astype(BF16)
    for hh in range(MLA_HEADS):
        base = hh * MLA_QK_DIM
        kcat_ref[:, base: base + LANES] = kvf[:, hh * LANES: (hh + 1) * LANES].astype(BF16)
        kcat_ref[:, base + LANES: base + 2 * LANES] = kext
    for hh in range(MLA_HEADS):
        sl = slice(nope_w + hh * LANES, nope_w + (hh + 1) * LANES)
        vt_ref[0, hh * LANES:(hh + 1) * LANES, :] = kvf[:, sl].T.astype(BF16)


def _in_proj(x2, B, S, p):
    T = B * S
    tm = TM_IN
    ns = S // tm
    row = lambda i: (i, 0)
    pos = lambda i: (i % ns, 0)
    tpose = lambda i: (i // ns, 0, i % ns)
    bf = lambda n: jax.ShapeDtypeStruct((T, n), BF16)
    bft = jax.ShapeDtypeStruct((B, RET_WIDTH, S), BF16)
    return pl.pallas_call(
        _in_proj_kernel,
        grid=(T // tm,),
        in_specs=[
            pl.BlockSpec((tm, D_MODEL), row),
            _const_spec((1, D_MODEL)),
            _const_spec((D_MODEL, IN_COLS)),
            _const_spec((D_MODEL, LANES)),
            _const_spec((1, MLA_Q_RANK)),
            _const_spec((1, MLA_KV_RANK)),
            _const_spec((MLA_Q_RANK, MLA_HEADS * (MLA_NOPE_DIM + MLA_ROPE_DIM))),
            _const_spec((MLA_KV_RANK, MLA_HEADS * (MLA_NOPE_DIM + MLA_V_DIM))),
            pl.BlockSpec((tm, LANES), pos),
            pl.BlockSpec((tm, LANES), pos),
            pl.BlockSpec((tm, LANES), pos),
            pl.BlockSpec((tm, LANES), pos),
        ],
        out_specs=[
            pl.BlockSpec((tm, RET_WIDTH), row),
            pl.BlockSpec((1, RET_WIDTH, tm), tpose),
            pl.BlockSpec((tm, RET_WIDTH), row),
            pl.BlockSpec((tm, RET_WIDTH), row),
            pl.BlockSpec((tm, MLA_HEADS * MLA_QK_DIM), row),
            pl.BlockSpec((tm, MLA_HEADS * MLA_QK_DIM), row),
            pl.BlockSpec((1, MLA_WIDTH, tm), tpose),
        ],
        out_shape=[bf(RET_WIDTH), bft, bf(RET_WIDTH), bf(RET_WIDTH),
                   bf(MLA_HEADS * MLA_QK_DIM), bf(MLA_HEADS * MLA_QK_DIM), bft],
        compiler_params=pltpu.CompilerParams(
            dimension_semantics=("parallel",), vmem_limit_bytes=VMEM_LIMIT_BYTES),
        name="in_proj",
    )(x2, p["g_mix_pre"], p["w_in"], p["w_kr"], p["g_q"], p["g_kv"], p["w_q_up"], p["w_kv_up"],
      p["cos_r"], p["sin_r"], p["cos_m"], p["sin_m"])


def _log_sigmoid(x):
    return jnp.minimum(x, 0.0) - jnp.log(1.0 + jnp.exp(-jnp.abs(x)))


def _retention_kernel(q_ref, kt_ref, v_ref, g_ref, lf_ref, lb_ref, o_ref, st_ref, *, n_chunks):
    C = RET_CHUNK
    lf = _log_sigmoid(lf_ref[0])
    lb = _log_sigmoid(lb_ref[0])
    row = lax.broadcasted_iota(jnp.int32, (C, C), 0).astype(F32)
    col = lax.broadcasted_iota(jnp.int32, (C, C), 1).astype(F32)
    diff = row - col
    decay = jnp.exp(jnp.where(diff >= 0, lf * diff, -lb * diff))
    wkf_t = jnp.exp(lf * (C - 1.0 - col))
    wkb_t = jnp.exp(lb * col)
    wqf = jnp.exp(lf * (row + 1.0))
    wqb = jnp.exp(lb * (C - row))
    df = jnp.exp(lf * C)
    db = jnp.exp(lb * C)

    def chunk_kv(c, w_t):
        off = pl.multiple_of(c * C, C)
        kt = kt_ref[0, :, pl.ds(off, C)].astype(F32)
        return jnp.dot((kt * w_t).astype(BF16), v_ref[0, pl.ds(off, C), :], preferred_element_type=F32)

    def scan_states(t, states):
        sf, sb = states
        cb = n_chunks - 1 - t
        st_ref[t, 0:C, :] = sf.astype(BF16)
        st_ref[cb, C:2 * C, :] = sb.astype(BF16)
        return sf * df + chunk_kv(t, wkf_t), sb * db + chunk_kv(cb, wkb_t)

    zero = jnp.zeros((C, C), F32)
    lax.fori_loop(0, n_chunks, scan_states, (zero, zero), unroll=RET_UNROLL)

    def out_chunk(c, carry):
        off = pl.multiple_of(c * C, C)
        q = q_ref[0, pl.ds(off, C), :]
        v = v_ref[0, pl.ds(off, C), :]
        s = jnp.dot(q, kt_ref[0, :, pl.ds(off, C)], preferred_element_type=F32)
        o = jnp.dot((s * decay).astype(BF16), v, preferred_element_type=F32)
        qf = q.astype(F32)
        qq = jnp.concatenate([(qf * wqf).astype(BF16), (qf * wqb).astype(BF16)], axis=1)
        o = o + jnp.dot(qq, st_ref[c], preferred_element_type=F32)
        o = o * lax.rsqrt(jnp.mean(o * o, axis=-1, keepdims=True) + NORM_EPS)
        o_ref[0, pl.ds(off, C), :] = (o * g_ref[0, pl.ds(off, C), :].astype(F32)).astype(BF16)
        return carry

    lax.fori_loop(0, n_chunks, out_chunk, 0, unroll=RET_UNROLL)


def _retention(rq, rkt, rv, gate, lf, lb, B, S):
    n = S // RET_CHUNK
    tok = pl.BlockSpec((1, S, LANES), lambda b, h: (b, 0, h))
    tok_t = pl.BlockSpec((1, LANES, S), lambda b, h: (b, h, 0))
    dec = pl.BlockSpec((1, 1, LANES), lambda b, h: (h, 0, 0))
    return pl.pallas_call(
        functools.partial(_retention_kernel, n_chunks=n),
        grid=(B, RET_HEADS),
        in_specs=[tok, tok_t, tok, tok, dec, dec],
        out_specs=tok,
        out_shape=jax.ShapeDtypeStruct((B, S, RET_WIDTH), BF16),
        scratch_shapes=[pltpu.VMEM((n, 2 * RET_CHUNK, RET_HEAD_DIM), BF16)],
        compiler_params=pltpu.CompilerParams(
            dimension_semantics=("parallel", "parallel"), vmem_limit_bytes=VMEM_LIMIT_BYTES),
        name="retention",
    )(rq.reshape(B, S, RET_WIDTH), rkt, rv.reshape(B, S, RET_WIDTH), gate.reshape(B, S, RET_WIDTH), lf, lb)


def _attention_kernel(q_ref, k_ref, vt_ref, o_ref, acc_ref, s_ref, mb_ref, *, n_q, n_kv, tq, tk):
    def produce(qi, j, slot):
        qoff = pl.multiple_of(qi * tq, tq)
        koff = pl.multiple_of(j * tk, tk)
        s = lax.dot_general(k_ref[0, pl.ds(koff, tk), :], q_ref[0, pl.ds(qoff, tq), :],
                            (((1,), (1,)), ((), ())), preferred_element_type=F32)
        s_ref[slot] = s
        mb_ref[slot] = jnp.max(s, axis=0, keepdims=True)

    def consume(j, slot, m, l):
        koff = pl.multiple_of(j * tk, tk)
        m_new = jnp.maximum(m, mb_ref[slot])
        alpha = jnp.exp2(m - m_new)
        p = jnp.exp2(s_ref[slot] - m_new)
        l_new = alpha * l + jnp.sum(p, axis=0, keepdims=True)
        pv = jnp.dot(vt_ref[0, :, pl.ds(koff, tk)], p.astype(BF16), preferred_element_type=F32)
        acc_ref[...] = alpha * acc_ref[...] + pv
        return m_new, l_new

    def query_block(qi, carry, last_q):
        def group(g, ml, last):
            m, l = ml
            j0 = g * KV_UNROLL
            for i in range(KV_UNROLL):
                if not (last and i == KV_UNROLL - 1):
                    produce(qi, j0 + i + 1, (i + 1) % 2)
                elif not last_q:
                    produce(qi + 1, 0, 0)
                m, l = consume(j0 + i, i % 2, m, l)
            return m, l

        n_groups = n_kv // KV_UNROLL
        acc_ref[...] = jnp.zeros_like(acc_ref)
        ml = (jnp.full((1, tq), -1e30, F32), jnp.zeros((1, tq), F32))
        if n_groups > 1:
            ml = lax.fori_loop(0, n_groups - 1, functools.partial(group, last=False), ml)
        _, l = group(n_groups - 1, ml, last=True)
        qoff = pl.multiple_of(qi * tq, tq)
        o_ref[0, pl.ds(qoff, tq), :] = (acc_ref[...] / l).T.astype(BF16)
        return carry

    produce(0, 0, 0)
    if n_q > 1:
        lax.fori_loop(0, n_q - 1, functools.partial(query_block, last_q=False), 0)
    query_block(n_q - 1, 0, last_q=True)


def _attention(qcat, kcat, vt, B, S):
    tq, tk = TQ_ATT, TK_ATT
    qk_spec = pl.BlockSpec((1, S, MLA_QK_DIM), lambda b, h: (b, 0, h))
    return pl.pallas_call(
        functools.partial(_attention_kernel, n_q=S // tq, n_kv=S // tk, tq=tq, tk=tk),
        grid=(B, MLA_HEADS),
        in_specs=[qk_spec, qk_spec, pl.BlockSpec((1, MLA_V_DIM, S), lambda b, h: (b, h, 0))],
        out_specs=pl.BlockSpec((1, S, MLA_V_DIM), lambda b, h: (b, 0, h)),
        out_shape=jax.ShapeDtypeStruct((B, S, MLA_WIDTH), BF16),
        scratch_shapes=[pltpu.VMEM((MLA_V_DIM, tq), F32), pltpu.VMEM((2, tk, tq), F32),
                        pltpu.VMEM((2, 1, tq), F32)],
        compiler_params=pltpu.CompilerParams(
            dimension_semantics=("parallel", "parallel"), vmem_limit_bytes=VMEM_LIMIT_BYTES),
        name="mla_attention",
    )(qcat.reshape(B, S, MLA_HEADS * MLA_QK_DIM), kcat.reshape(B, S, MLA_HEADS * MLA_QK_DIM), vt)


def _out_proj_kernel(ret_ref, mla_ref, x_ref, w_ref, g_ref, o_ref):
    mix = jnp.dot(ret_ref[...], w_ref[0:RET_WIDTH, :], preferred_element_type=F32)
    mix = mix + jnp.dot(mla_ref[...], w_ref[RET_WIDTH:, :], preferred_element_type=F32)
    o_ref[...] = x_ref[...] + _rms(mix, g_ref[...])


def _out_proj(o_ret, o_mla, x2, p):
    T = x2.shape[0]
    tm = TM_OUT
    row = lambda i: (i, 0)
    return pl.pallas_call(
        _out_proj_kernel,
        grid=(T // tm,),
        in_specs=[
            pl.BlockSpec((tm, RET_WIDTH), row),
            pl.BlockSpec((tm, MLA_WIDTH), row),
            pl.BlockSpec((tm, D_MODEL), row),
            _const_spec((RET_WIDTH + MLA_WIDTH, D_MODEL)),
            _const_spec((1, D_MODEL)),
        ],
        out_specs=pl.BlockSpec((tm, D_MODEL), row),
        out_shape=jax.ShapeDtypeStruct((T, D_MODEL), F32),
        compiler_params=pltpu.CompilerParams(
            dimension_semantics=("parallel",), vmem_limit_bytes=VMEM_LIMIT_BYTES),
        name="out_proj",
    )(o_ret.reshape(T, RET_WIDTH), o_mla.reshape(T, MLA_WIDTH), x2, p["w_out"], p["g_mix_post"])


def _ffn_kernel(x_ref, gpre_ref, wup_ref, wdn_ref, gpost_ref, o_ref, h_ref):
    j = pl.program_id(1)

    @pl.when(j == 0)
    def _():
        h_ref[...] = _rms(x_ref[...], gpre_ref[...]).astype(BF16)
        o_ref[...] = jnp.zeros_like(o_ref)

    a = jnp.maximum(jnp.dot(h_ref[...], wup_ref[...], preferred_element_type=F32), 0.0)
    o_ref[...] += jnp.dot((a * a).astype(BF16), wdn_ref[...], preferred_element_type=F32)

    @pl.when(j == pl.num_programs(1) - 1)
    def _():
        o_ref[...] = x_ref[...] + _rms(o_ref[...], gpost_ref[...])


def _ffn(x1, p):
    T = x1.shape[0]
    tm, tf = TM_FFN, TF_FFN
    return pl.pallas_call(
        _ffn_kernel,
        grid=(T // tm, FFN_DIM // tf),
        in_specs=[
            pl.BlockSpec((tm, D_MODEL), lambda i, j: (i, 0)),
            _const_spec((1, D_MODEL)),
            pl.BlockSpec((D_MODEL, tf), lambda i, j: (0, j)),
            pl.BlockSpec((tf, D_MODEL), lambda i, j: (j, 0)),
            _const_spec((1, D_MODEL)),
        ],
        out_specs=pl.BlockSpec((tm, D_MODEL), lambda i, j: (i, 0)),
        out_shape=jax.ShapeDtypeStruct((T, D_MODEL), F32),
        scratch_shapes=[pltpu.VMEM((tm, D_MODEL), BF16)],
        compiler_params=pltpu.CompilerParams(
            dimension_semantics=("parallel", "arbitrary"), vmem_limit_bytes=VMEM_LIMIT_BYTES),
        name="ffn",
    )(x1, p["g_ffn_pre"], p["w_ffn_up"], p["w_ffn_down"], p["g_ffn_post"])


def _rope_tables(S):
    lo_n = 64

    def cos_sin(d):
        inv = (ROPE_BASE ** (-jnp.arange(0, d, 2, dtype=F32) / d))[None, :]
        a_hi = jnp.arange(0, S, lo_n, dtype=F32)[:, None] * inv
        a_lo = jnp.arange(lo_n, dtype=F32)[:, None] * inv
        ch, sh = jnp.cos(a_hi)[:, None, :], jnp.sin(a_hi)[:, None, :]
        cl, sl = jnp.cos(a_lo)[None, :, :], jnp.sin(a_lo)[None, :, :]
        return (ch * cl - sh * sl).reshape(S, d // 2), (sh * cl + ch * sl).reshape(S, d // 2)

    c_r, s_r = cos_sin(RET_HEAD_DIM)
    c_m, s_m = cos_sin(MLA_ROPE_DIM)
    cos_r = jnp.concatenate([c_r] * 2, axis=1)
    sin_r = jnp.concatenate([-s_r, s_r], axis=1)
    cos_m = jnp.concatenate([c_m] * 4, axis=1)
    sin_m = jnp.concatenate([-s_m] * 2 + [s_m] * 2, axis=1)
    return cos_r, sin_r, cos_m, sin_m


def _layout_params(norm_mix_pre, w_in, ret_decay_fwd, ret_decay_bwd, mla_q_norm, w_q_up, mla_kv_norm, w_kv_up,
                   w_out, norm_mix_post, norm_ffn_pre, w_ffn_up, w_ffn_down, norm_ffn_post):
    half = MLA_ROPE_DIM // 2
    main = 4 * RET_WIDTH + MLA_Q_RANK + MLA_KV_RANK
    kr1, kr2 = w_in[:, main: main + half], w_in[:, main + half:]
    w_kr = jnp.concatenate([kr1, kr1, kr2, kr2], axis=1).astype(BF16)

    per_head = MLA_NOPE_DIM + MLA_ROPE_DIM
    wq = w_q_up.reshape(MLA_Q_RANK, MLA_HEADS, per_head)
    nope = wq[:, :, :MLA_NOPE_DIM].reshape(MLA_Q_RANK, MLA_HEADS * MLA_NOPE_DIM)
    x1 = wq[:, :, MLA_NOPE_DIM: MLA_NOPE_DIM + half].reshape(MLA_Q_RANK, MLA_HEADS // 2, 2 * half)
    x2 = wq[:, :, MLA_NOPE_DIM + half:].reshape(MLA_Q_RANK, MLA_HEADS // 2, 2 * half)
    slabs = jnp.concatenate([x1, x2], axis=2).reshape(MLA_Q_RANK, MLA_HEADS * MLA_ROPE_DIM)
    w_q_p = jnp.concatenate([nope, slabs], axis=1).astype(BF16)

    wkv = w_kv_up.reshape(MLA_KV_RANK, MLA_HEADS, MLA_NOPE_DIM + MLA_V_DIM)
    w_kv_p = jnp.concatenate([
        wkv[:, :, :MLA_NOPE_DIM].reshape(MLA_KV_RANK, MLA_HEADS * MLA_NOPE_DIM),
        wkv[:, :, MLA_NOPE_DIM:].reshape(MLA_KV_RANK, MLA_HEADS * MLA_V_DIM)], axis=1).astype(BF16)

    bcast = lambda v: jnp.broadcast_to(v.astype(F32)[:, None, None], (RET_HEADS, 1, LANES))
    return {
        "g_mix_pre": norm_mix_pre.reshape(1, D_MODEL), "w_in": w_in.astype(BF16), "w_kr": w_kr,
        "lf": bcast(ret_decay_fwd), "lb": bcast(ret_decay_bwd),
        "g_q": mla_q_norm.reshape(1, MLA_Q_RANK), "w_q_up": w_q_p,
        "g_kv": mla_kv_norm.reshape(1, MLA_KV_RANK), "w_kv_up": w_kv_p,
        "w_out": w_out.astype(BF16), "g_mix_post": norm_mix_post.reshape(1, D_MODEL),
        "g_ffn_pre": norm_ffn_pre.reshape(1, D_MODEL), "w_ffn_up": w_ffn_up.astype(BF16),
        "w_ffn_down": w_ffn_down.astype(BF16), "g_ffn_post": norm_ffn_post.reshape(1, D_MODEL),
    }


def _layer(x, p):
    B, S, _ = x.shape
    x2 = x.reshape(B * S, D_MODEL)
    rq, rkt, rv, gate, qcat, kcat, vt = _in_proj(x2, B, S, p)
    o_ret = _retention(rq, rkt, rv, gate, p["lf"], p["lb"], B, S)
    o_mla = _attention(qcat, kcat, vt, B, S)
    x1 = _out_proj(o_ret, o_mla, x2, p)
    return _ffn(x1, p).reshape(B, S, D_MODEL)


def kernel(x_prompt, x_sample, norm_mix_pre, w_in, ret_decay_fwd, ret_decay_bwd, mla_q_norm, w_q_up, mla_kv_norm,
           w_kv_up, w_out, norm_mix_post, norm_ffn_pre, w_ffn_up, w_ffn_down, norm_ffn_post):
    y_prompt, y_sample = x_prompt, x_sample
    tables = dict(zip(("cos_r", "sin_r", "cos_m", "sin_m"), _rope_tables(max(x_prompt.shape[1], x_sample.shape[1]))))
    for l in range(norm_mix_pre.shape[0]):
        p = _layout_params(norm_mix_pre[l], w_in[l], ret_decay_fwd[l], ret_decay_bwd[l], mla_q_norm[l], w_q_up[l],
                           mla_kv_norm[l], w_kv_up[l], w_out[l], norm_mix_post[l], norm_ffn_pre[l], w_ffn_up[l],
                           w_ffn_down[l], norm_ffn_post[l])
        p.update(tables)
        y_prompt = _layer(y_prompt, p)
        y_sample = _layer(y_sample, p)
    return (y_prompt, y_sample)
```

```python
import functools
import math

import jax
import jax.numpy as jnp
from jax import lax
from jax.experimental import pallas as pl
from jax.experimental.pallas import tpu as pltpu

D_MODEL = 2048
RET_HEADS = 8
RET_HEAD_DIM = 128
RET_WIDTH = RET_HEADS * RET_HEAD_DIM
RET_CHUNK = 128
MLA_HEADS = 8
MLA_NOPE_DIM = 128
MLA_ROPE_DIM = 64
MLA_V_DIM = 128
MLA_Q_RANK = 512
MLA_KV_RANK = 512
MLA_WIDTH = MLA_HEADS * MLA_V_DIM
FFN_DIM = 4 * D_MODEL
ROPE_BASE = 10000.0
NORM_EPS = 1e-6

LANES = 128
MLA_QK_DIM = 2 * LANES
IN_COLS = 4 * RET_WIDTH + MLA_Q_RANK + MLA_KV_RANK + MLA_ROPE_DIM
VMEM_LIMIT_BYTES = 60000 * 1024

F32 = jnp.float32
Q_SCALE_LOG2 = (MLA_NOPE_DIM + MLA_ROPE_DIM) ** -0.5 * math.log2(math.e)
BF16 = jnp.bfloat16

TM_IN = 256
TM_OUT = 512
TM_FFN = 1024
TF_FFN = 1024
NORM_ROWS = 16
TQ_ATT = 1024
TK_ATT = 1024
RET_BLOCK = 256
RET_UNROLL = 8
KV_UNROLL = 2


def _const_spec(shape):
    return pl.BlockSpec(shape, lambda *_: (0,) * len(shape), pipeline_mode=pl.Buffered(1))


def _rms(x, gain):
    return x * lax.rsqrt(jnp.mean(x * x, axis=-1, keepdims=True) + NORM_EPS) * gain


def _rope_slab(x, cos, sin_signed):
    return x * cos + pltpu.roll(x, LANES // 2, 1) * sin_signed


def _in_proj_kernel(x_ref, g_ref, w_ref, wkr_ref, qg_ref, kvg_ref, wq_ref, wkv_ref, cr_ref, sr_ref, cm_ref, sm_ref,
                    rq_ref, rkt_ref, rv_ref, gate_ref, qcat_ref, kcat_ref, vt_ref):
    h = _rms(x_ref[...], g_ref[...]).astype(BF16)

    def proj(lo, hi):
        return jnp.dot(h, w_ref[:, lo:hi], preferred_element_type=F32)

    cr, sr = cr_ref[...], sr_ref[...]
    cm, sm = cm_ref[...], sm_ref[...]

    off = 4 * RET_WIDTH
    cq = _rms(proj(off, off + MLA_Q_RANK), qg_ref[...]).astype(BF16)
    qf = jnp.dot(cq, wq_ref[...], preferred_element_type=F32) * Q_SCALE_LOG2
    lane = lax.broadcasted_iota(jnp.int32, (1, LANES), 1)
    even_lanes = (lane // (MLA_ROPE_DIM // 2)) % 2 == 0
    nope_w = MLA_HEADS * MLA_NOPE_DIM
    for j in range(MLA_HEADS // 2):
        slab = _rope_slab(qf[:, nope_w + j * LANES: nope_w + (j + 1) * LANES], cm, sm)
        for par in range(2):
            hh = 2 * j + par
            base = hh * MLA_QK_DIM
            qcat_ref[:, base: base + LANES] = qf[:, hh * LANES: (hh + 1) * LANES].astype(BF16)
            keep = even_lanes if par == 0 else jnp.logical_not(even_lanes)
            qcat_ref[:, base + LANES: base + 2 * LANES] = jnp.where(keep, slab, 0.0).astype(BF16)

    off += MLA_Q_RANK
    ckv = _rms(proj(off, off + MLA_KV_RANK), kvg_ref[...]).astype(BF16)
    kvf = jnp.dot(ckv, wkv_ref[...], preferred_element_type=F32)
    kext = _rope_slab(jnp.dot(h, wkr_ref[...], preferred_element_type=F32), cm, sm).astype(BF16)
    for hh in range(MLA_HEADS):
        base = hh * MLA_QK_DIM
        kcat_ref[:, base: base + LANES] = kvf[:, hh * LANES: (hh + 1) * LANES].astype(BF16)
        kcat_ref[:, base + LANES: base + 2 * LANES] = kext
    for hh in range(MLA_HEADS):
        sl = slice(nope_w + hh * LANES, nope_w + (hh + 1) * LANES)
        vt_ref[0, hh * LANES:(hh + 1) * LANES, :] = kvf[:, sl].T.astype(BF16)

    q = proj(0, RET_WIDTH)
    for hh in range(RET_HEADS):
        sl = slice(hh * LANES, (hh + 1) * LANES)
        rq_ref[:, sl] = _rope_slab(q[:, sl], cr, sr).astype(BF16)

    k = proj(RET_WIDTH, 2 * RET_WIDTH)
    k_scale = RET_HEAD_DIM ** -0.5
    for hh in range(RET_HEADS):
        sl = slice(hh * LANES, (hh + 1) * LANES)
        rkt_ref[0, sl, :] = (_rope_slab(k[:, sl], cr, sr) * k_scale).T.astype(BF16)

    rv_ref[...] = proj(2 * RET_WIDTH, 3 * RET_WIDTH).astype(BF16)

    g = proj(3 * RET_WIDTH, 4 * RET_WIDTH)
    gate_ref[...] = (g / (1.0 + jnp.exp(-g))).astype(BF16)


def _in_proj(x2, B, S, p):
    T = B * S
    tm = TM_IN
    ns = S // tm
    row = lambda i: (i, 0)
    pos = lambda i: (i % ns, 0)
    tpose = lambda i: (i // ns, 0, i % ns)
    bf = lambda n: jax.ShapeDtypeStruct((T, n), BF16)
    bft = jax.ShapeDtypeStruct((B, RET_WIDTH, S), BF16)
    return pl.pallas_call(
        _in_proj_kernel,
        grid=(T // tm,),
        in_specs=[
            pl.BlockSpec((tm, D_MODEL), row),
            _const_spec((1, D_MODEL)),
            _const_spec((D_MODEL, IN_COLS)),
            _const_spec((D_MODEL, LANES)),
            _const_spec((1, MLA_Q_RANK)),
            _const_spec((1, MLA_KV_RANK)),
            _const_spec((MLA_Q_RANK, MLA_HEADS * (MLA_NOPE_DIM + MLA_ROPE_DIM))),
            _const_spec((MLA_KV_RANK, MLA_HEADS * (MLA_NOPE_DIM + MLA_V_DIM))),
            pl.BlockSpec((tm, LANES), pos),
            pl.BlockSpec((tm, LANES), pos),
            pl.BlockSpec((tm, LANES), pos),
            pl.BlockSpec((tm, LANES), pos),
        ],
        out_specs=[
            pl.BlockSpec((tm, RET_WIDTH), row),
            pl.BlockSpec((1, RET_WIDTH, tm), tpose),
            pl.BlockSpec((tm, RET_WIDTH), row),
            pl.BlockSpec((tm, RET_WIDTH), row),
            pl.BlockSpec((tm, MLA_HEADS * MLA_QK_DIM), row),
            pl.BlockSpec((tm, MLA_HEADS * MLA_QK_DIM), row),
            pl.BlockSpec((1, MLA_WIDTH, tm), tpose),
        ],
        out_shape=[bf(RET_WIDTH), bft, bf(RET_WIDTH), bf(RET_WIDTH),
                   bf(MLA_HEADS * MLA_QK_DIM), bf(MLA_HEADS * MLA_QK_DIM), bft],
        compiler_params=pltpu.CompilerParams(
            dimension_semantics=("parallel",), vmem_limit_bytes=VMEM_LIMIT_BYTES),
        name="in_proj",
    )(x2, p["g_mix_pre"], p["w_in"], p["w_kr"], p["g_q"], p["g_kv"], p["w_q_up"], p["w_kv_up"],
      p["cos_r"], p["sin_r"], p["cos_m"], p["sin_m"])


def _log_sigmoid(x):
    return jnp.minimum(x, 0.0) - jnp.log(1.0 + jnp.exp(-jnp.abs(x)))


def _retention_kernel(q_ref, kt_ref, v_ref, g_ref, lf_ref, lb_ref, o_ref, st_ref, *, n_chunks):
    C, D = RET_BLOCK, RET_HEAD_DIM
    lf = _log_sigmoid(lf_ref[0])
    lb = _log_sigmoid(lb_ref[0])
    lf1, lb1 = lf[:, 0:1], lb[:, 0:1]

    def iota(shape, dim):
        return lax.broadcasted_iota(jnp.int32, shape, dim).astype(F32)

    diff = iota((C, C), 0) - iota((C, C), 1)
    decay = jnp.exp(jnp.where(diff >= 0, lf1 * diff, -lb1 * diff))
    pos_t = iota((D, C), 1)
    wkf_t = jnp.exp(lf1 * (C - 1.0 - pos_t))
    wkb_t = jnp.exp(lb1 * pos_t)
    pos = iota((C, D), 0)
    wqf = jnp.exp(lf * (pos + 1.0))
    wqb = jnp.exp(lb * (C - pos))
    df = jnp.exp(lf * C)
    db = jnp.exp(lb * C)

    def chunk_kv(c, w_t):
        off = pl.multiple_of(c * C, C)
        kt = kt_ref[0, :, pl.ds(off, C)].astype(F32)
        return jnp.dot((kt * w_t).astype(BF16), v_ref[0, pl.ds(off, C), :], preferred_element_type=F32)

    def scan_states(t, states):
        sf, sb = states
        cb = n_chunks - 1 - t
        st_ref[t, 0:D, :] = sf.astype(BF16)
        st_ref[cb, D:2 * D, :] = sb.astype(BF16)
        return sf * df + chunk_kv(t, wkf_t), sb * db + chunk_kv(cb, wkb_t)

    zero = jnp.zeros((D, D), F32)
    lax.fori_loop(0, n_chunks, scan_states, (zero, zero), unroll=RET_UNROLL)

    def out_chunk(c, carry):
        off = pl.multiple_of(c * C, C)
        q = q_ref[0, pl.ds(off, C), :]
        v = v_ref[0, pl.ds(off, C), :]
        s = jnp.dot(q, kt_ref[0, :, pl.ds(off, C)], preferred_element_type=F32)
        o = jnp.dot((s * decay).astype(BF16), v, preferred_element_type=F32)
        qf = q.astype(F32)
        qq = jnp.concatenate([(qf * wqf).astype(BF16), (qf * wqb).astype(BF16)], axis=1)
        o = o + jnp.dot(qq, st_ref[c], preferred_element_type=F32)
        o = o * lax.rsqrt(jnp.mean(o * o, axis=-1, keepdims=True) + NORM_EPS)
        o_ref[0, pl.ds(off, C), :] = (o * g_ref[0, pl.ds(off, C), :].astype(F32)).astype(BF16)
        return carry

    lax.fori_loop(0, n_chunks, out_chunk, 0, unroll=RET_UNROLL)


def _retention(rq, rkt, rv, gate, lf, lb, B, S):
    n = S // RET_BLOCK
    tok = pl.BlockSpec((1, S, LANES), lambda b, h: (b, 0, h))
    tok_t = pl.BlockSpec((1, LANES, S), lambda b, h: (b, h, 0))
    dec = pl.BlockSpec((1, 1, LANES), lambda b, h: (h, 0, 0))
    return pl.pallas_call(
        functools.partial(_retention_kernel, n_chunks=n),
        grid=(B, RET_HEADS),
        in_specs=[tok, tok_t, tok, tok, dec, dec],
        out_specs=tok,
        out_shape=jax.ShapeDtypeStruct((B, S, RET_WIDTH), BF16),
        scratch_shapes=[pltpu.VMEM((n, 2 * RET_HEAD_DIM, RET_HEAD_DIM), BF16)],
        compiler_params=pltpu.CompilerParams(
            dimension_semantics=("parallel", "parallel"), vmem_limit_bytes=VMEM_LIMIT_BYTES),
        name="retention",
    )(rq.reshape(B, S, RET_WIDTH), rkt, rv.reshape(B, S, RET_WIDTH), gate.reshape(B, S, RET_WIDTH), lf, lb)


def _attention_kernel(q_ref, k_ref, vt_ref, o_ref, acc_ref, s_ref, mb_ref, *, n_q, n_kv, tq, tk):
    def produce(qi, j, slot):
        qoff = pl.multiple_of(qi * tq, tq)
        koff = pl.multiple_of(j * tk, tk)
        s = lax.dot_general(k_ref[0, pl.ds(koff, tk), :], q_ref[0, pl.ds(qoff, tq), :],
                            (((1,), (1,)), ((), ())), preferred_element_type=F32)
        s_ref[slot] = s
        mb_ref[slot] = jnp.max(s, axis=0, keepdims=True)

    def consume(j, slot, m, l):
        koff = pl.multiple_of(j * tk, tk)
        m_new = jnp.maximum(m, mb_ref[slot])
        alpha = jnp.exp2(m - m_new)
        p = jnp.exp2(s_ref[slot] - m_new)
        l_new = alpha * l + jnp.sum(p, axis=0, keepdims=True)
        pv = jnp.dot(vt_ref[0, :, pl.ds(koff, tk)], p.astype(BF16), preferred_element_type=F32)
        acc_ref[...] = alpha * acc_ref[...] + pv
        return m_new, l_new

    def query_block(qi, carry, last_q):
        def group(g, ml, last):
            m, l = ml
            j0 = g * KV_UNROLL
            for i in range(KV_UNROLL):
                if not (last and i == KV_UNROLL - 1):
                    produce(qi, j0 + i + 1, (i + 1) % 2)
                elif not last_q:
                    produce(qi + 1, 0, 0)
                m, l = consume(j0 + i, i % 2, m, l)
            return m, l

        n_groups = n_kv // KV_UNROLL
        acc_ref[...] = jnp.zeros_like(acc_ref)
        ml = (jnp.full((1, tq), -1e30, F32), jnp.zeros((1, tq), F32))
        if n_groups > 1:
            ml = lax.fori_loop(0, n_groups - 1, functools.partial(group, last=False), ml)
        _, l = group(n_groups - 1, ml, last=True)
        qoff = pl.multiple_of(qi * tq, tq)
        o_ref[0, pl.ds(qoff, tq), :] = (acc_ref[...] / l).T.astype(BF16)
        return carry

    produce(0, 0, 0)
    if n_q > 1:
        lax.fori_loop(0, n_q - 1, functools.partial(query_block, last_q=False), 0)
    query_block(n_q - 1, 0, last_q=True)


def _attention(qcat, kcat, vt, B, S):
    tq, tk = TQ_ATT, TK_ATT
    qk_spec = pl.BlockSpec((1, S, MLA_QK_DIM), lambda b, h: (b, 0, h))
    return pl.pallas_call(
        functools.partial(_attention_kernel, n_q=S // tq, n_kv=S // tk, tq=tq, tk=tk),
        grid=(B, MLA_HEADS),
        in_specs=[qk_spec, qk_spec, pl.BlockSpec((1, MLA_V_DIM, S), lambda b, h: (b, h, 0))],
        out_specs=pl.BlockSpec((1, S, MLA_V_DIM), lambda b, h: (b, 0, h)),
        out_shape=jax.ShapeDtypeStruct((B, S, MLA_WIDTH), BF16),
        scratch_shapes=[pltpu.VMEM((MLA_V_DIM, tq), F32), pltpu.VMEM((2, tk, tq), F32),
                        pltpu.VMEM((2, 1, tq), F32)],
        compiler_params=pltpu.CompilerParams(
            dimension_semantics=("parallel", "parallel"), vmem_limit_bytes=VMEM_LIMIT_BYTES),
        name="mla_attention",
    )(qcat.reshape(B, S, MLA_HEADS * MLA_QK_DIM), kcat.reshape(B, S, MLA_HEADS * MLA_QK_DIM), vt)


def _out_proj_kernel(ret_ref, mla_ref, x_ref, w_ref, g_ref, o_ref):
    mix = jnp.dot(ret_ref[...], w_ref[0:RET_WIDTH, :], preferred_element_type=F32)
    mix = mix + jnp.dot(mla_ref[...], w_ref[RET_WIDTH:, :], preferred_element_type=F32)
    o_ref[...] = x_ref[...] + _rms(mix, g_ref[...])


def _out_proj(o_ret, o_mla, x2, p):
    T = x2.shape[0]
    tm = TM_OUT
    row = lambda i: (i, 0)
    return pl.pallas_call(
        _out_proj_kernel,
        grid=(T // tm,),
        in_specs=[
            pl.BlockSpec((tm, RET_WIDTH), row),
            pl.BlockSpec((tm, MLA_WIDTH), row),
            pl.BlockSpec((tm, D_MODEL), row),
            _const_spec((RET_WIDTH + MLA_WIDTH, D_MODEL)),
            _const_spec((1, D_MODEL)),
        ],
        out_specs=pl.BlockSpec((tm, D_MODEL), row),
        out_shape=jax.ShapeDtypeStruct((T, D_MODEL), F32),
        compiler_params=pltpu.CompilerParams(
            dimension_semantics=("parallel",), vmem_limit_bytes=VMEM_LIMIT_BYTES),
        name="out_proj",
    )(o_ret.reshape(T, RET_WIDTH), o_mla.reshape(T, MLA_WIDTH), x2, p["w_out"], p["g_mix_post"])


def _ffn_kernel(x_ref, gpre_ref, wup_ref, wdn_ref, gpost_ref, o_ref, h_ref):
    j = pl.program_id(1)

    n_chunks = x_ref.shape[0] // NORM_ROWS

    @pl.when(j == 0)
    def _():
        def pre(r, carry):
            rows = pl.ds(pl.multiple_of(r * NORM_ROWS, NORM_ROWS), NORM_ROWS)
            h_ref[rows, :] = _rms(x_ref[rows, :], gpre_ref[...]).astype(BF16)
            o_ref[rows, :] = jnp.zeros((NORM_ROWS, D_MODEL), F32)
            return carry
        lax.fori_loop(0, n_chunks, pre, 0, unroll=8)

    a = jnp.maximum(jnp.dot(h_ref[...], wup_ref[...], preferred_element_type=F32), 0.0)
    o_ref[...] += jnp.dot((a * a).astype(BF16), wdn_ref[...], preferred_element_type=F32)

    @pl.when(j == pl.num_programs(1) - 1)
    def _():
        for r in range(n_chunks):
            rows = slice(r * NORM_ROWS, (r + 1) * NORM_ROWS)
            o_ref[rows, :] = x_ref[rows, :] + _rms(o_ref[rows, :], gpost_ref[...])


def _ffn(x1, p):
    T = x1.shape[0]
    tm, tf = TM_FFN, TF_FFN
    return pl.pallas_call(
        _ffn_kernel,
        grid=(T // tm, FFN_DIM // tf),
        in_specs=[
            pl.BlockSpec((tm, D_MODEL), lambda i, j: (i, 0)),
            _const_spec((1, D_MODEL)),
            pl.BlockSpec((D_MODEL, tf), lambda i, j: (0, j)),
            pl.BlockSpec((tf, D_MODEL), lambda i, j: (j, 0)),
            _const_spec((1, D_MODEL)),
        ],
        out_specs=pl.BlockSpec((tm, D_MODEL), lambda i, j: (i, 0)),
        out_shape=jax.ShapeDtypeStruct((T, D_MODEL), F32),
        scratch_shapes=[pltpu.VMEM((tm, D_MODEL), BF16)],
        compiler_params=pltpu.CompilerParams(
            dimension_semantics=("parallel", "arbitrary"), vmem_limit_bytes=VMEM_LIMIT_BYTES),
        name="ffn",
    )(x1, p["g_ffn_pre"], p["w_ffn_up"], p["w_ffn_down"], p["g_ffn_post"])


def _rope_tables(S):
    lo_n = 64

    def cos_sin(d):
        inv = (ROPE_BASE ** (-jnp.arange(0, d, 2, dtype=F32) / d))[None, :]
        a_hi = jnp.arange(0, S, lo_n, dtype=F32)[:, None] * inv
        a_lo = jnp.arange(lo_n, dtype=F32)[:, None] * inv
        ch, sh = jnp.cos(a_hi)[:, None, :], jnp.sin(a_hi)[:, None, :]
        cl, sl = jnp.cos(a_lo)[None, :, :], jnp.sin(a_lo)[None, :, :]
        return (ch * cl - sh * sl).reshape(S, d // 2), (sh * cl + ch * sl).reshape(S, d // 2)

    c_r, s_r = cos_sin(RET_HEAD_DIM)
    c_m, s_m = cos_sin(MLA_ROPE_DIM)
    cos_r = jnp.concatenate([c_r] * 2, axis=1)
    sin_r = jnp.concatenate([-s_r, s_r], axis=1)
    cos_m = jnp.concatenate([c_m] * 4, axis=1)
    sin_m = jnp.concatenate([-s_m] * 2 + [s_m] * 2, axis=1)
    return cos_r, sin_r, cos_m, sin_m


def _layout_params(norm_mix_pre, w_in, ret_decay_fwd, ret_decay_bwd, mla_q_norm, w_q_up, mla_kv_norm, w_kv_up,
                   w_out, norm_mix_post, norm_ffn_pre, w_ffn_up, w_ffn_down, norm_ffn_post):
    half = MLA_ROPE_DIM // 2
    main = 4 * RET_WIDTH + MLA_Q_RANK + MLA_KV_RANK
    kr1, kr2 = w_in[:, main: main + half], w_in[:, main + half:]
    w_kr = jnp.concatenate([kr1, kr1, kr2, kr2], axis=1).astype(BF16)

    per_head = MLA_NOPE_DIM + MLA_ROPE_DIM
    wq = w_q_up.reshape(MLA_Q_RANK, MLA_HEADS, per_head)
    nope = wq[:, :, :MLA_NOPE_DIM].reshape(MLA_Q_RANK, MLA_HEADS * MLA_NOPE_DIM)
    x1 = wq[:, :, MLA_NOPE_DIM: MLA_NOPE_DIM + half].reshape(MLA_Q_RANK, MLA_HEADS // 2, 2 * half)
    x2 = wq[:, :, MLA_NOPE_DIM + half:].reshape(MLA_Q_RANK, MLA_HEADS // 2, 2 * half)
    slabs = jnp.concatenate([x1, x2], axis=2).reshape(MLA_Q_RANK, MLA_HEADS * MLA_ROPE_DIM)
    w_q_p = jnp.concatenate([nope, slabs], axis=1).astype(BF16)

    wkv = w_kv_up.reshape(MLA_KV_RANK, MLA_HEADS, MLA_NOPE_DIM + MLA_V_DIM)
    w_kv_p = jnp.concatenate([
        wkv[:, :, :MLA_NOPE_DIM].reshape(MLA_KV_RANK, MLA_HEADS * MLA_NOPE_DIM),
        wkv[:, :, MLA_NOPE_DIM:].reshape(MLA_KV_RANK, MLA_HEADS * MLA_V_DIM)], axis=1).astype(BF16)

    bcast = lambda v: jnp.broadcast_to(v.astype(F32)[:, None, None], (RET_HEADS, 1, LANES))
    return {
        "g_mix_pre": norm_mix_pre.reshape(1, D_MODEL), "w_in": w_in.astype(BF16), "w_kr": w_kr,
        "lf": bcast(ret_decay_fwd), "lb": bcast(ret_decay_bwd),
        "g_q": mla_q_norm.reshape(1, MLA_Q_RANK), "w_q_up": w_q_p,
        "g_kv": mla_kv_norm.reshape(1, MLA_KV_RANK), "w_kv_up": w_kv_p,
        "w_out": w_out.astype(BF16), "g_mix_post": norm_mix_post.reshape(1, D_MODEL),
        "g_ffn_pre": norm_ffn_pre.reshape(1, D_MODEL), "w_ffn_up": w_ffn_up.astype(BF16),
        "w_ffn_down": w_ffn_down.astype(BF16), "g_ffn_post": norm_ffn_post.reshape(1, D_MODEL),
    }


def _layer(x, p):
    B, S, _ = x.shape
    x2 = x.reshape(B * S, D_MODEL)
    rq, rkt, rv, gate, qcat, kcat, vt = _in_proj(x2, B, S, p)
    o_ret = _retention(rq, rkt, rv, gate, p["lf"], p["lb"], B, S)
    o_mla = _attention(qcat, kcat, vt, B, S)
    x1 = _out_proj(o_ret, o_mla, x2, p)
    return _ffn(x1, p).reshape(B, S, D_MODEL)


def kernel(x_prompt, x_sample, norm_mix_pre, w_in, ret_decay_fwd, ret_decay_bwd, mla_q_norm, w_q_up, mla_kv_norm,
           w_kv_up, w_out, norm_mix_post, norm_ffn_pre, w_ffn_up, w_ffn_down, norm_ffn_post):
    y_prompt, y_sample = x_prompt, x_sample
    tables = dict(zip(("cos_r", "sin_r", "cos_m", "sin_m"), _rope_tables(max(x_prompt.shape[1], x_sample.shape[1]))))
    for l in range(norm_mix_pre.shape[0]):
        p = _layout_params(norm_mix_pre[l], w_in[l], ret_decay_fwd[l], ret_decay_bwd[l], mla_q_norm[l], w_q_up[l],
                           mla_kv_norm[l], w_kv_up[l], w_out[l], norm_mix_post[l], norm_ffn_pre[l], w_ffn_up[l],
                           w_ffn_down[l], norm_ffn_post[l])
        p.update(tables)
        y_prompt = _layer(y_prompt, p)
        y_sample = _layer(y_sample, p)
    return (y_prompt, y_sample)
```

```python
import functools
import math

import jax
import jax.numpy as jnp
from jax import lax
from jax.experimental import pallas as pl
from jax.experimental.pallas import tpu as pltpu

D_MODEL = 2048
RET_HEADS = 8
RET_HEAD_DIM = 128
RET_WIDTH = RET_HEADS * RET_HEAD_DIM
RET_CHUNK = 128
MLA_HEADS = 8
MLA_NOPE_DIM = 128
MLA_ROPE_DIM = 64
MLA_V_DIM = 128
MLA_Q_RANK = 512
MLA_KV_RANK = 512
MLA_WIDTH = MLA_HEADS * MLA_V_DIM
FFN_DIM = 4 * D_MODEL
ROPE_BASE = 10000.0
NORM_EPS = 1e-6

LANES = 128
MLA_QK_DIM = 2 * LANES
BF16_SUBLANES = 16
V_EXT_ROWS = MLA_V_DIM + BF16_SUBLANES
IN_COLS = 4 * RET_WIDTH + MLA_Q_RANK + MLA_KV_RANK + MLA_ROPE_DIM
VMEM_LIMIT_BYTES = 60000 * 1024

F32 = jnp.float32
Q_SCALE_LOG2 = (MLA_NOPE_DIM + MLA_ROPE_DIM) ** -0.5 * math.log2(math.e)
BF16 = jnp.bfloat16

TM_IN = 256
TM_OUT = 512
TM_FFN = 1024
TF_FFN = 1024
NORM_ROWS = 16
NORM_UNROLL = 8
TQ_ATT = 1024
TK_ATT = 1024
RET_BLOCK = 256
RET_UNROLL = 8
KV_UNROLL = 2


def _const_spec(shape):
    return pl.BlockSpec(shape, lambda *_: (0,) * len(shape), pipeline_mode=pl.Buffered(1))


def _rms(x, gain):
    return x * lax.rsqrt(jnp.mean(x * x, axis=-1, keepdims=True) + NORM_EPS) * gain


def _rope_slab(x, cos, sin_signed):
    return x * cos + pltpu.roll(x, LANES // 2, 1) * sin_signed


def _in_proj_kernel(x_ref, g_ref, w_ref, wkr_ref, qg_ref, kvg_ref, wq_ref, wkv_ref, cr_ref, sr_ref, cm_ref, sm_ref,
                    rq_ref, rkt_ref, rv_ref, gate_ref, qcat_ref, kcat_ref, vt_ref):
    h = _rms(x_ref[...], g_ref[...]).astype(BF16)

    def proj(lo, hi):
        return jnp.dot(h, w_ref[:, lo:hi], preferred_element_type=F32)

    cr, sr = cr_ref[...], sr_ref[...]
    cm, sm = cm_ref[...], sm_ref[...]

    off = 4 * RET_WIDTH
    cq = _rms(proj(off, off + MLA_Q_RANK), qg_ref[...]).astype(BF16)
    qf = jnp.dot(cq, wq_ref[...], preferred_element_type=F32) * Q_SCALE_LOG2
    lane = lax.broadcasted_iota(jnp.int32, (1, LANES), 1)
    even_lanes = (lane // (MLA_ROPE_DIM // 2)) % 2 == 0
    nope_w = MLA_HEADS * MLA_NOPE_DIM
    for j in range(MLA_HEADS // 2):
        slab = _rope_slab(qf[:, nope_w + j * LANES: nope_w + (j + 1) * LANES], cm, sm)
        for par in range(2):
            hh = 2 * j + par
            base = hh * MLA_QK_DIM
            qcat_ref[:, base: base + LANES] = qf[:, hh * LANES: (hh + 1) * LANES].astype(BF16)
            keep = even_lanes if par == 0 else jnp.logical_not(even_lanes)
            qcat_ref[:, base + LANES: base + 2 * LANES] = jnp.where(keep, slab, 0.0).astype(BF16)

    off += MLA_Q_RANK
    ckv = _rms(proj(off, off + MLA_KV_RANK), kvg_ref[...]).astype(BF16)
    kvf = jnp.dot(ckv, wkv_ref[...], preferred_element_type=F32)
    kext = _rope_slab(jnp.dot(h, wkr_ref[...], preferred_element_type=F32), cm, sm).astype(BF16)
    for hh in range(MLA_HEADS):
        base = hh * MLA_QK_DIM
        kcat_ref[:, base: base + LANES] = kvf[:, hh * LANES: (hh + 1) * LANES].astype(BF16)
        kcat_ref[:, base + LANES: base + 2 * LANES] = kext
    for hh in range(MLA_HEADS):
        sl = slice(nope_w + hh * LANES, nope_w + (hh + 1) * LANES)
        vt_ref[0, hh * V_EXT_ROWS: hh * V_EXT_ROWS + MLA_V_DIM, :] = kvf[:, sl].T.astype(BF16)
        vt_ref[0, hh * V_EXT_ROWS + MLA_V_DIM: (hh + 1) * V_EXT_ROWS, :] = jnp.ones(
            (BF16_SUBLANES, x_ref.shape[0]), BF16)

    q = proj(0, RET_WIDTH)
    for hh in range(RET_HEADS):
        sl = slice(hh * LANES, (hh + 1) * LANES)
        rq_ref[:, sl] = _rope_slab(q[:, sl], cr, sr).astype(BF16)

    k = proj(RET_WIDTH, 2 * RET_WIDTH)
    k_scale = RET_HEAD_DIM ** -0.5
    for hh in range(RET_HEADS):
        sl = slice(hh * LANES, (hh + 1) * LANES)
        rkt_ref[0, sl, :] = (_rope_slab(k[:, sl], cr, sr) * k_scale).T.astype(BF16)

    rv_ref[...] = proj(2 * RET_WIDTH, 3 * RET_WIDTH).astype(BF16)

    g = proj(3 * RET_WIDTH, 4 * RET_WIDTH)
    gate_ref[...] = (g / (1.0 + jnp.exp(-g))).astype(BF16)


def _in_proj(x2, B, S, p):
    T = B * S
    tm = TM_IN
    assert S % tm == 0, (S, tm)
    ns = S // tm
    row = lambda i: (i, 0)
    pos = lambda i: (i % ns, 0)
    tpose = lambda i: (i // ns, 0, i % ns)
    bf = lambda n: jax.ShapeDtypeStruct((T, n), BF16)
    bft = jax.ShapeDtypeStruct((B, RET_WIDTH, S), BF16)
    bfv = jax.ShapeDtypeStruct((B, MLA_HEADS * V_EXT_ROWS, S), BF16)
    return pl.pallas_call(
        _in_proj_kernel,
        grid=(T // tm,),
        in_specs=[
            pl.BlockSpec((tm, D_MODEL), row),
            _const_spec((1, D_MODEL)),
            _const_spec((D_MODEL, IN_COLS)),
            _const_spec((D_MODEL, LANES)),
            _const_spec((1, MLA_Q_RANK)),
            _const_spec((1, MLA_KV_RANK)),
            _const_spec((MLA_Q_RANK, MLA_HEADS * (MLA_NOPE_DIM + MLA_ROPE_DIM))),
            _const_spec((MLA_KV_RANK, MLA_HEADS * (MLA_NOPE_DIM + MLA_V_DIM))),
            pl.BlockSpec((tm, LANES), pos),
            pl.BlockSpec((tm, LANES), pos),
            pl.BlockSpec((tm, LANES), pos),
            pl.BlockSpec((tm, LANES), pos),
        ],
        out_specs=[
            pl.BlockSpec((tm, RET_WIDTH), row),
            pl.BlockSpec((1, RET_WIDTH, tm), tpose),
            pl.BlockSpec((tm, RET_WIDTH), row),
            pl.BlockSpec((tm, RET_WIDTH), row),
            pl.BlockSpec((tm, MLA_HEADS * MLA_QK_DIM), row),
            pl.BlockSpec((tm, MLA_HEADS * MLA_QK_DIM), row),
            pl.BlockSpec((1, MLA_HEADS * V_EXT_ROWS, tm), tpose),
        ],
        out_shape=[bf(RET_WIDTH), bft, bf(RET_WIDTH), bf(RET_WIDTH),
                   bf(MLA_HEADS * MLA_QK_DIM), bf(MLA_HEADS * MLA_QK_DIM), bfv],
        compiler_params=pltpu.CompilerParams(
            dimension_semantics=("parallel",), vmem_limit_bytes=VMEM_LIMIT_BYTES),
        name="in_proj",
    )(x2, p["g_mix_pre"], p["w_in"], p["w_kr"], p["g_q"], p["g_kv"], p["w_q_up"], p["w_kv_up"],
      p["cos_r"], p["sin_r"], p["cos_m"], p["sin_m"])


def _log_sigmoid(x):
    return jnp.minimum(x, 0.0) - jnp.log(1.0 + jnp.exp(-jnp.abs(x)))


def _retention_kernel(q_ref, kt_ref, v_ref, g_ref, lf_ref, lb_ref, o_ref, st_ref, *, n_chunks):
    C, D = RET_BLOCK, RET_HEAD_DIM
    lf = _log_sigmoid(lf_ref[0])
    lb = _log_sigmoid(lb_ref[0])
    lf1, lb1 = lf[:, 0:1], lb[:, 0:1]

    def iota(shape, dim):
        return lax.broadcasted_iota(jnp.int32, shape, dim).astype(F32)

    diff = iota((C, C), 0) - iota((C, C), 1)
    decay = jnp.exp(jnp.where(diff >= 0, lf1 * diff, -lb1 * diff))
    pos_t = iota((D, C), 1)
    wkf_t = jnp.exp(lf1 * (C - 1.0 - pos_t))
    wkb_t = jnp.exp(lb1 * pos_t)
    pos = iota((C, D), 0)
    wqf = jnp.exp(lf * (pos + 1.0))
    wqb = jnp.exp(lb * (C - pos))
    df = jnp.exp(lf * C)
    db = jnp.exp(lb * C)

    def chunk_kv(c, w_t):
        off = pl.multiple_of(c * C, C)
        kt = kt_ref[0, :, pl.ds(off, C)].astype(F32)
        return jnp.dot((kt * w_t).astype(BF16), v_ref[0, pl.ds(off, C), :], preferred_element_type=F32)

    def scan_states(t, states):
        sf, sb = states
        cb = n_chunks - 1 - t
        st_ref[t, 0:D, :] = sf.astype(BF16)
        st_ref[cb, D:2 * D, :] = sb.astype(BF16)
        return sf * df + chunk_kv(t, wkf_t), sb * db + chunk_kv(cb, wkb_t)

    zero = jnp.zeros((D, D), F32)
    lax.fori_loop(0, n_chunks, scan_states, (zero, zero), unroll=RET_UNROLL)

    def out_chunk(c, carry):
        off = pl.multiple_of(c * C, C)
        q = q_ref[0, pl.ds(off, C), :]
        v = v_ref[0, pl.ds(off, C), :]
        s = jnp.dot(q, kt_ref[0, :, pl.ds(off, C)], preferred_element_type=F32)
        o = jnp.dot((s * decay).astype(BF16), v, preferred_element_type=F32)
        qf = q.astype(F32)
        qq = jnp.concatenate([(qf * wqf).astype(BF16), (qf * wqb).astype(BF16)], axis=1)
        o = o + jnp.dot(qq, st_ref[c], preferred_element_type=F32)
        o = o * lax.rsqrt(jnp.mean(o * o, axis=-1, keepdims=True) + NORM_EPS)
        o_ref[0, pl.ds(off, C), :] = (o * g_ref[0, pl.ds(off, C), :].astype(F32)).astype(BF16)
        return carry

    lax.fori_loop(0, n_chunks, out_chunk, 0, unroll=RET_UNROLL)


def _retention(rq, rkt, rv, gate, lf, lb, B, S):
    assert S % RET_BLOCK == 0, (S, RET_BLOCK)
    n = S // RET_BLOCK
    tok = pl.BlockSpec((1, S, LANES), lambda b, h: (b, 0, h))
    tok_t = pl.BlockSpec((1, LANES, S), lambda b, h: (b, h, 0))
    dec = pl.BlockSpec((1, 1, LANES), lambda b, h: (h, 0, 0))
    return pl.pallas_call(
        functools.partial(_retention_kernel, n_chunks=n),
        grid=(B, RET_HEADS),
        in_specs=[tok, tok_t, tok, tok, dec, dec],
        out_specs=tok,
        out_shape=jax.ShapeDtypeStruct((B, S, RET_WIDTH), BF16),
        scratch_shapes=[pltpu.VMEM((n, 2 * RET_HEAD_DIM, RET_HEAD_DIM), BF16)],
        compiler_params=pltpu.CompilerParams(
            dimension_semantics=("parallel", "parallel"), vmem_limit_bytes=VMEM_LIMIT_BYTES),
        name="retention",
    )(rq.reshape(B, S, RET_WIDTH), rkt, rv.reshape(B, S, RET_WIDTH), gate.reshape(B, S, RET_WIDTH), lf, lb)


def _attention_kernel(q_ref, k_ref, vt_ref, o_ref, acc_ref, s_ref, mb_ref, *, n_q, n_kv, tq, tk):
    def produce(qi, j, slot):
        qoff = pl.multiple_of(qi * tq, tq)
        koff = pl.multiple_of(j * tk, tk)
        s = lax.dot_general(k_ref[0, pl.ds(koff, tk), :], q_ref[0, pl.ds(qoff, tq), :],
                            (((1,), (1,)), ((), ())), preferred_element_type=F32)
        s_ref[slot] = s
        mb_ref[slot] = jnp.max(s, axis=0, keepdims=True)

    def consume(j, slot, m):
        koff = pl.multiple_of(j * tk, tk)
        m_new = jnp.maximum(m, mb_ref[slot])
        alpha = jnp.exp2(m - m_new)
        p = jnp.exp2(s_ref[slot] - m_new).astype(BF16)
        pv = jnp.dot(vt_ref[0, :, pl.ds(koff, tk)], p, preferred_element_type=F32)
        acc_ref[...] = alpha * acc_ref[...] + pv
        return m_new

    def query_block(qi, carry, last_q):
        def group(g, m, last):
            j0 = g * KV_UNROLL
            for i in range(KV_UNROLL):
                if not (last and i == KV_UNROLL - 1):
                    produce(qi, j0 + i + 1, (i + 1) % 2)
                elif not last_q:
                    produce(qi + 1, 0, 0)
                m = consume(j0 + i, i % 2, m)
            return m

        n_groups = n_kv // KV_UNROLL
        acc_ref[...] = jnp.zeros_like(acc_ref)
        m = jnp.full((1, tq), -jnp.inf, F32)
        if n_groups > 1:
            m = lax.fori_loop(0, n_groups - 1, functools.partial(group, last=False), m)
        group(n_groups - 1, m, last=True)
        qoff = pl.multiple_of(qi * tq, tq)
        acc = acc_ref[...]
        o = acc[0:MLA_V_DIM, :] / acc[MLA_V_DIM:MLA_V_DIM + 1, :]
        o_ref[0, pl.ds(qoff, tq), :] = o.T.astype(BF16)
        return carry

    produce(0, 0, 0)
    if n_q > 1:
        lax.fori_loop(0, n_q - 1, functools.partial(query_block, last_q=False), 0)
    query_block(n_q - 1, 0, last_q=True)


def _attention(qcat, kcat, vt, B, S):
    tq, tk = TQ_ATT, TK_ATT
    assert S % tq == 0 and S % tk == 0 and KV_UNROLL % 2 == 0 and (S // tk) % KV_UNROLL == 0, (S, tq, tk)
    qk_spec = pl.BlockSpec((1, S, MLA_QK_DIM), lambda b, h: (b, 0, h))
    return pl.pallas_call(
        functools.partial(_attention_kernel, n_q=S // tq, n_kv=S // tk, tq=tq, tk=tk),
        grid=(B, MLA_HEADS),
        in_specs=[qk_spec, qk_spec, pl.BlockSpec((1, V_EXT_ROWS, S), lambda b, h: (b, h, 0))],
        out_specs=pl.BlockSpec((1, S, MLA_V_DIM), lambda b, h: (b, 0, h)),
        out_shape=jax.ShapeDtypeStruct((B, S, MLA_WIDTH), BF16),
        scratch_shapes=[pltpu.VMEM((V_EXT_ROWS, tq), F32), pltpu.VMEM((2, tk, tq), F32),
                        pltpu.VMEM((2, 1, tq), F32)],
        compiler_params=pltpu.CompilerParams(
            dimension_semantics=("parallel", "parallel"), vmem_limit_bytes=VMEM_LIMIT_BYTES),
        name="mla_attention",
    )(qcat.reshape(B, S, MLA_HEADS * MLA_QK_DIM), kcat.reshape(B, S, MLA_HEADS * MLA_QK_DIM), vt)


def _out_proj_kernel(ret_ref, mla_ref, x_ref, w_ref, g_ref, o_ref):
    mix = jnp.dot(ret_ref[...], w_ref[0:RET_WIDTH, :], preferred_element_type=F32)
    mix = mix + jnp.dot(mla_ref[...], w_ref[RET_WIDTH:, :], preferred_element_type=F32)
    o_ref[...] = x_ref[...] + _rms(mix, g_ref[...])


def _out_proj(o_ret, o_mla, x2, p):
    T = x2.shape[0]
    tm = TM_OUT
    assert T % tm == 0, (T, tm)
    row = lambda i: (i, 0)
    return pl.pallas_call(
        _out_proj_kernel,
        grid=(T // tm,),
        in_specs=[
            pl.BlockSpec((tm, RET_WIDTH), row),
            pl.BlockSpec((tm, MLA_WIDTH), row),
            pl.BlockSpec((tm, D_MODEL), row),
            _const_spec((RET_WIDTH + MLA_WIDTH, D_MODEL)),
            _const_spec((1, D_MODEL)),
        ],
        out_specs=pl.BlockSpec((tm, D_MODEL), row),
        out_shape=jax.ShapeDtypeStruct((T, D_MODEL), F32),
        compiler_params=pltpu.CompilerParams(
            dimension_semantics=("parallel",), vmem_limit_bytes=VMEM_LIMIT_BYTES),
        name="out_proj",
    )(o_ret.reshape(T, RET_WIDTH), o_mla.reshape(T, MLA_WIDTH), x2, p["w_out"], p["g_mix_post"])


def _ffn_kernel(x_ref, gpre_ref, wup_ref, wdn_ref, gpost_ref, o_ref, h_ref):
    j = pl.program_id(1)

    n_chunks = x_ref.shape[0] // NORM_ROWS

    @pl.when(j == 0)
    def _():
        def pre(r, carry):
            rows = pl.ds(pl.multiple_of(r * NORM_ROWS, NORM_ROWS), NORM_ROWS)
            h_ref[rows, :] = _rms(x_ref[rows, :], gpre_ref[...]).astype(BF16)
            o_ref[rows, :] = jnp.zeros((NORM_ROWS, D_MODEL), F32)
            return carry
        lax.fori_loop(0, n_chunks, pre, 0, unroll=NORM_UNROLL)

    a = jnp.maximum(jnp.dot(h_ref[...], wup_ref[...], preferred_element_type=F32), 0.0)
    o_ref[...] += jnp.dot((a * a).astype(BF16), wdn_ref[...], preferred_element_type=F32)

    @pl.when(j == pl.num_programs(1) - 1)
    def _():
        for r in range(n_chunks):
            rows = slice(r * NORM_ROWS, (r + 1) * NORM_ROWS)
            o_ref[rows, :] = x_ref[rows, :] + _rms(o_ref[rows, :], gpost_ref[...])


def _ffn(x1, p):
    T = x1.shape[0]
    tm, tf = TM_FFN, TF_FFN
    assert T % tm == 0 and FFN_DIM % tf == 0 and tm % NORM_ROWS == 0, (T, tm, tf)
    return pl.pallas_call(
        _ffn_kernel,
        grid=(T // tm, FFN_DIM // tf),
        in_specs=[
            pl.BlockSpec((tm, D_MODEL), lambda i, j: (i, 0)),
            _const_spec((1, D_MODEL)),
            pl.BlockSpec((D_MODEL, tf), lambda i, j: (0, j)),
            pl.BlockSpec((tf, D_MODEL), lambda i, j: (j, 0)),
            _const_spec((1, D_MODEL)),
        ],
        out_specs=pl.BlockSpec((tm, D_MODEL), lambda i, j: (i, 0)),
        out_shape=jax.ShapeDtypeStruct((T, D_MODEL), F32),
        scratch_shapes=[pltpu.VMEM((tm, D_MODEL), BF16)],
        compiler_params=pltpu.CompilerParams(
            dimension_semantics=("parallel", "arbitrary"), vmem_limit_bytes=VMEM_LIMIT_BYTES),
        name="ffn",
    )(x1, p["g_ffn_pre"], p["w_ffn_up"], p["w_ffn_down"], p["g_ffn_post"])


def _rope_tables(S):
    lo_n = 64

    def cos_sin(d):
        inv = (ROPE_BASE ** (-jnp.arange(0, d, 2, dtype=F32) / d))[None, :]
        a_hi = jnp.arange(0, S, lo_n, dtype=F32)[:, None] * inv
        a_lo = jnp.arange(lo_n, dtype=F32)[:, None] * inv
        ch, sh = jnp.cos(a_hi)[:, None, :], jnp.sin(a_hi)[:, None, :]
        cl, sl = jnp.cos(a_lo)[None, :, :], jnp.sin(a_lo)[None, :, :]
        return (ch * cl - sh * sl).reshape(S, d // 2), (sh * cl + ch * sl).reshape(S, d // 2)

    c_r, s_r = cos_sin(RET_HEAD_DIM)
    c_m, s_m = cos_sin(MLA_ROPE_DIM)
    cos_r = jnp.concatenate([c_r] * 2, axis=1)
    sin_r = jnp.concatenate([-s_r, s_r], axis=1)
    cos_m = jnp.concatenate([c_m] * 4, axis=1)
    sin_m = jnp.concatenate([-s_m] * 2 + [s_m] * 2, axis=1)
    return cos_r, sin_r, cos_m, sin_m


def _layout_params(norm_mix_pre, w_in, ret_decay_fwd, ret_decay_bwd, mla_q_norm, w_q_up, mla_kv_norm, w_kv_up,
                   w_out, norm_mix_post, norm_ffn_pre, w_ffn_up, w_ffn_down, norm_ffn_post):
    half = MLA_ROPE_DIM // 2
    main = 4 * RET_WIDTH + MLA_Q_RANK + MLA_KV_RANK
    kr1, kr2 = w_in[:, main: main + half], w_in[:, main + half:]
    w_kr = jnp.concatenate([kr1, kr1, kr2, kr2], axis=1).astype(BF16)

    per_head = MLA_NOPE_DIM + MLA_ROPE_DIM
    wq = w_q_up.reshape(MLA_Q_RANK, MLA_HEADS, per_head)
    nope = wq[:, :, :MLA_NOPE_DIM].reshape(MLA_Q_RANK, MLA_HEADS * MLA_NOPE_DIM)
    x1 = wq[:, :, MLA_NOPE_DIM: MLA_NOPE_DIM + half].reshape(MLA_Q_RANK, MLA_HEADS // 2, 2 * half)
    x2 = wq[:, :, MLA_NOPE_DIM + half:].reshape(MLA_Q_RANK, MLA_HEADS // 2, 2 * half)
    slabs = jnp.concatenate([x1, x2], axis=2).reshape(MLA_Q_RANK, MLA_HEADS * MLA_ROPE_DIM)
    w_q_p = jnp.concatenate([nope, slabs], axis=1).astype(BF16)

    wkv = w_kv_up.reshape(MLA_KV_RANK, MLA_HEADS, MLA_NOPE_DIM + MLA_V_DIM)
    w_kv_p = jnp.concatenate([
        wkv[:, :, :MLA_NOPE_DIM].reshape(MLA_KV_RANK, MLA_HEADS * MLA_NOPE_DIM),
        wkv[:, :, MLA_NOPE_DIM:].reshape(MLA_KV_RANK, MLA_HEADS * MLA_V_DIM)], axis=1).astype(BF16)

    bcast = lambda v: jnp.broadcast_to(v.astype(F32)[:, None, None], (RET_HEADS, 1, LANES))
    return {
        "g_mix_pre": norm_mix_pre.reshape(1, D_MODEL), "w_in": w_in.astype(BF16), "w_kr": w_kr,
        "lf": bcast(ret_decay_fwd), "lb": bcast(ret_decay_bwd),
        "g_q": mla_q_norm.reshape(1, MLA_Q_RANK), "w_q_up": w_q_p,
        "g_kv": mla_kv_norm.reshape(1, MLA_KV_RANK), "w_kv_up": w_kv_p,
        "w_out": w_out.astype(BF16), "g_mix_post": norm_mix_post.reshape(1, D_MODEL),
        "g_ffn_pre": norm_ffn_pre.reshape(1, D_MODEL), "w_ffn_up": w_ffn_up.astype(BF16),
        "w_ffn_down": w_ffn_down.astype(BF16), "g_ffn_post": norm_ffn_post.reshape(1, D_MODEL),
    }


def _layer(x, p):
    B, S, _ = x.shape
    x2 = x.reshape(B * S, D_MODEL)
    rq, rkt, rv, gate, qcat, kcat, vt = _in_proj(x2, B, S, p)
    o_ret = _retention(rq, rkt, rv, gate, p["lf"], p["lb"], B, S)
    o_mla = _attention(qcat, kcat, vt, B, S)
    x1 = _out_proj(o_ret, o_mla, x2, p)
    return _ffn(x1, p).reshape(B, S, D_MODEL)


def kernel(x_prompt, x_sample, norm_mix_pre, w_in, ret_decay_fwd, ret_decay_bwd, mla_q_norm, w_q_up, mla_kv_norm,
           w_kv_up, w_out, norm_mix_post, norm_ffn_pre, w_ffn_up, w_ffn_down, norm_ffn_post):
    y_prompt, y_sample = x_prompt, x_sample
    tables = dict(zip(("cos_r", "sin_r", "cos_m", "sin_m"), _rope_tables(max(x_prompt.shape[1], x_sample.shape[1]))))
    for l in range(norm_mix_pre.shape[0]):
        p = _layout_params(norm_mix_pre[l], w_in[l], ret_decay_fwd[l], ret_decay_bwd[l], mla_q_norm[l], w_q_up[l],
                           mla_kv_norm[l], w_kv_up[l], w_out[l], norm_mix_post[l], norm_ffn_pre[l], w_ffn_up[l],
                           w_ffn_down[l], norm_ffn_post[l])
        p.update(tables)
        y_prompt = _layer(y_prompt, p)
        y_sample = _layer(y_sample, p)
    return (y_prompt, y_sample)
```

```python
import functools
import math

import jax
import jax.numpy as jnp
from jax import lax
from jax.experimental import pallas as pl
from jax.experimental.pallas import tpu as pltpu

D_MODEL = 2048
RET_HEADS = 8
RET_HEAD_DIM = 128
RET_WIDTH = RET_HEADS * RET_HEAD_DIM
RET_CHUNK = 128
MLA_HEADS = 8
MLA_NOPE_DIM = 128
MLA_ROPE_DIM = 64
MLA_V_DIM = 128
MLA_Q_RANK = 512
MLA_KV_RANK = 512
MLA_WIDTH = MLA_HEADS * MLA_V_DIM
FFN_DIM = 4 * D_MODEL
ROPE_BASE = 10000.0
NORM_EPS = 1e-6

LANES = 128
MLA_QK_DIM = 2 * LANES
BF16_SUBLANES = 16
V_EXT_ROWS = MLA_V_DIM + BF16_SUBLANES
IN_COLS = 4 * RET_WIDTH + MLA_Q_RANK + MLA_KV_RANK + MLA_ROPE_DIM
VMEM_LIMIT_BYTES = 60000 * 1024

F32 = jnp.float32
Q_SCALE_LOG2 = (MLA_NOPE_DIM + MLA_ROPE_DIM) ** -0.5 * math.log2(math.e)
BF16 = jnp.bfloat16

TM_IN = 256
TM_OUT = 512
TM_FFN = 1024
TF_FFN = 1024
NORM_ROWS = 16
NORM_UNROLL = 8
TQ_ATT = 1024
TK_ATT = 2048
RET_BLOCK = 256
RET_UNROLL = 8
KV_UNROLL = 2


def _const_spec(shape):
    return pl.BlockSpec(shape, lambda *_: (0,) * len(shape), pipeline_mode=pl.Buffered(1))


def _rms(x, gain):
    return x * lax.rsqrt(jnp.mean(x * x, axis=-1, keepdims=True) + NORM_EPS) * gain


def _rope_slab(x, cos, sin_signed):
    return x * cos + pltpu.roll(x, LANES // 2, 1) * sin_signed


def _in_proj_kernel(x_ref, g_ref, w_ref, wkr_ref, qg_ref, kvg_ref, wq_ref, wkv_ref, cr_ref, sr_ref, cm_ref, sm_ref,
                    rq_ref, rkt_ref, rv_ref, gate_ref, qcat_ref, kcat_ref, vt_ref):
    x = x_ref[...]
    h = (x * g_ref[...]).astype(BF16)
    r = lax.rsqrt(jnp.mean(x * x, axis=-1, keepdims=True) + NORM_EPS)

    def proj(lo, hi):
        return jnp.dot(h, w_ref[:, lo:hi], preferred_element_type=F32) * r

    cr, sr = cr_ref[...], sr_ref[...]
    cm, sm = cm_ref[...], sm_ref[...]

    off = 4 * RET_WIDTH
    cq = _rms(proj(off, off + MLA_Q_RANK), qg_ref[...]).astype(BF16)
    qf = jnp.dot(cq, wq_ref[...], preferred_element_type=F32) * Q_SCALE_LOG2
    lane = lax.broadcasted_iota(jnp.int32, (1, LANES), 1)
    even_lanes = (lane // (MLA_ROPE_DIM // 2)) % 2 == 0
    nope_w = MLA_HEADS * MLA_NOPE_DIM
    for j in range(MLA_HEADS // 2):
        slab = _rope_slab(qf[:, nope_w + j * LANES: nope_w + (j + 1) * LANES], cm, sm)
        for par in range(2):
            hh = 2 * j + par
            base = hh * MLA_QK_DIM
            qcat_ref[:, base: base + LANES] = qf[:, hh * LANES: (hh + 1) * LANES].astype(BF16)
            keep = even_lanes if par == 0 else jnp.logical_not(even_lanes)
            qcat_ref[:, base + LANES: base + 2 * LANES] = jnp.where(keep, slab, 0.0).astype(BF16)

    off += MLA_Q_RANK
    ckv = _rms(proj(off, off + MLA_KV_RANK), kvg_ref[...]).astype(BF16)
    kvf = jnp.dot(ckv, wkv_ref[...], preferred_element_type=F32)
    kext = _rope_slab(jnp.dot(h, wkr_ref[...], preferred_element_type=F32) * r, cm, sm).astype(BF16)
    for hh in range(MLA_HEADS):
        base = hh * MLA_QK_DIM
        kcat_ref[:, base: base + LANES] = kvf[:, hh * LANES: (hh + 1) * LANES].astype(BF16)
        kcat_ref[:, base + LANES: base + 2 * LANES] = kext
    for hh in range(MLA_HEADS):
        sl = slice(nope_w + hh * LANES, nope_w + (hh + 1) * LANES)
        vt_ref[0, hh * V_EXT_ROWS: hh * V_EXT_ROWS + MLA_V_DIM, :] = kvf[:, sl].T.astype(BF16)
        vt_ref[0, hh * V_EXT_ROWS + MLA_V_DIM: (hh + 1) * V_EXT_ROWS, :] = jnp.ones(
            (BF16_SUBLANES, x_ref.shape[0]), BF16)

    q = proj(0, RET_WIDTH)
    for hh in range(RET_HEADS):
        sl = slice(hh * LANES, (hh + 1) * LANES)
        rq_ref[:, sl] = _rope_slab(q[:, sl], cr, sr).astype(BF16)

    k = proj(RET_WIDTH, 2 * RET_WIDTH)
    k_scale = RET_HEAD_DIM ** -0.5
    for hh in range(RET_HEADS):
        sl = slice(hh * LANES, (hh + 1) * LANES)
        rkt_ref[0, sl, :] = (_rope_slab(k[:, sl], cr, sr) * k_scale).T.astype(BF16)

    rv_ref[...] = proj(2 * RET_WIDTH, 3 * RET_WIDTH).astype(BF16)

    g = proj(3 * RET_WIDTH, 4 * RET_WIDTH)
    gate_ref[...] = (g / (1.0 + jnp.exp(-g))).astype(BF16)


def _in_proj(x2, B, S, p):
    T = B * S
    tm = TM_IN
    assert S % tm == 0, (S, tm)
    ns = S // tm
    row = lambda i: (i, 0)
    pos = lambda i: (i % ns, 0)
    tpose = lambda i: (i // ns, 0, i % ns)
    bf = lambda n: jax.ShapeDtypeStruct((T, n), BF16)
    bft = jax.ShapeDtypeStruct((B, RET_WIDTH, S), BF16)
    bfv = jax.ShapeDtypeStruct((B, MLA_HEADS * V_EXT_ROWS, S), BF16)
    return pl.pallas_call(
        _in_proj_kernel,
        grid=(T // tm,),
        in_specs=[
            pl.BlockSpec((tm, D_MODEL), row),
            _const_spec((1, D_MODEL)),
            _const_spec((D_MODEL, IN_COLS)),
            _const_spec((D_MODEL, LANES)),
            _const_spec((1, MLA_Q_RANK)),
            _const_spec((1, MLA_KV_RANK)),
            _const_spec((MLA_Q_RANK, MLA_HEADS * (MLA_NOPE_DIM + MLA_ROPE_DIM))),
            _const_spec((MLA_KV_RANK, MLA_HEADS * (MLA_NOPE_DIM + MLA_V_DIM))),
            pl.BlockSpec((tm, LANES), pos),
            pl.BlockSpec((tm, LANES), pos),
            pl.BlockSpec((tm, LANES), pos),
            pl.BlockSpec((tm, LANES), pos),
        ],
        out_specs=[
            pl.BlockSpec((tm, RET_WIDTH), row),
            pl.BlockSpec((1, RET_WIDTH, tm), tpose),
            pl.BlockSpec((tm, RET_WIDTH), row),
            pl.BlockSpec((tm, RET_WIDTH), row),
            pl.BlockSpec((tm, MLA_HEADS * MLA_QK_DIM), row),
            pl.BlockSpec((tm, MLA_HEADS * MLA_QK_DIM), row),
            pl.BlockSpec((1, MLA_HEADS * V_EXT_ROWS, tm), tpose),
        ],
        out_shape=[bf(RET_WIDTH), bft, bf(RET_WIDTH), bf(RET_WIDTH),
                   bf(MLA_HEADS * MLA_QK_DIM), bf(MLA_HEADS * MLA_QK_DIM), bfv],
        compiler_params=pltpu.CompilerParams(
            dimension_semantics=("parallel",), vmem_limit_bytes=VMEM_LIMIT_BYTES),
        name="in_proj",
    )(x2, p["g_mix_pre"], p["w_in"], p["w_kr"], p["g_q"], p["g_kv"], p["w_q_up"], p["w_kv_up"],
      p["cos_r"], p["sin_r"], p["cos_m"], p["sin_m"])


def _log_sigmoid(x):
    return jnp.minimum(x, 0.0) - jnp.log(1.0 + jnp.exp(-jnp.abs(x)))


def _retention_kernel(q_ref, kt_ref, v_ref, g_ref, lf_ref, lb_ref, o_ref, st_ref, *, n_chunks):
    C, D = RET_BLOCK, RET_HEAD_DIM
    lf = _log_sigmoid(lf_ref[0])
    lb = _log_sigmoid(lb_ref[0])
    lf1, lb1 = lf[:, 0:1], lb[:, 0:1]

    def iota(shape, dim):
        return lax.broadcasted_iota(jnp.int32, shape, dim).astype(F32)

    diff = iota((C, C), 0) - iota((C, C), 1)
    decay = jnp.exp(jnp.where(diff >= 0, lf1 * diff, -lb1 * diff))
    pos_t = iota((D, C), 1)
    wkf_t = jnp.exp(lf1 * (C - 1.0 - pos_t))
    wkb_t = jnp.exp(lb1 * pos_t)
    pos = iota((C, D), 0)
    wqf = jnp.exp(lf * (pos + 1.0))
    wqb = jnp.exp(lb * (C - pos))
    df = jnp.exp(lf * C)
    db = jnp.exp(lb * C)

    def chunk_kv(c, w_t):
        off = pl.multiple_of(c * C, C)
        kt = kt_ref[0, :, pl.ds(off, C)].astype(F32)
        return jnp.dot((kt * w_t).astype(BF16), v_ref[0, pl.ds(off, C), :], preferred_element_type=F32)

    def scan_states(t, states):
        sf, sb = states
        cb = n_chunks - 1 - t
        st_ref[t, 0:D, :] = sf.astype(BF16)
        st_ref[cb, D:2 * D, :] = sb.astype(BF16)
        return sf * df + chunk_kv(t, wkf_t), sb * db + chunk_kv(cb, wkb_t)

    zero = jnp.zeros((D, D), F32)
    lax.fori_loop(0, n_chunks, scan_states, (zero, zero), unroll=RET_UNROLL)

    def out_chunk(c, carry):
        off = pl.multiple_of(c * C, C)
        q = q_ref[0, pl.ds(off, C), :]
        v = v_ref[0, pl.ds(off, C), :]
        s = jnp.dot(q, kt_ref[0, :, pl.ds(off, C)], preferred_element_type=F32)
        o = jnp.dot((s * decay).astype(BF16), v, preferred_element_type=F32)
        qf = q.astype(F32)
        qq = jnp.concatenate([(qf * wqf).astype(BF16), (qf * wqb).astype(BF16)], axis=1)
        o = o + jnp.dot(qq, st_ref[c], preferred_element_type=F32)
        o = o * lax.rsqrt(jnp.mean(o * o, axis=-1, keepdims=True) + NORM_EPS)
        o_ref[0, pl.ds(off, C), :] = (o * g_ref[0, pl.ds(off, C), :].astype(F32)).astype(BF16)
        return carry

    lax.fori_loop(0, n_chunks, out_chunk, 0, unroll=RET_UNROLL)


def _retention(rq, rkt, rv, gate, lf, lb, B, S):
    assert S % RET_BLOCK == 0, (S, RET_BLOCK)
    n = S // RET_BLOCK
    tok = pl.BlockSpec((1, S, LANES), lambda b, h: (b, 0, h))
    tok_t = pl.BlockSpec((1, LANES, S), lambda b, h: (b, h, 0))
    dec = pl.BlockSpec((1, 1, LANES), lambda b, h: (h, 0, 0))
    return pl.pallas_call(
        functools.partial(_retention_kernel, n_chunks=n),
        grid=(B, RET_HEADS),
        in_specs=[tok, tok_t, tok, tok, dec, dec],
        out_specs=tok,
        out_shape=jax.ShapeDtypeStruct((B, S, RET_WIDTH), BF16),
        scratch_shapes=[pltpu.VMEM((n, 2 * RET_HEAD_DIM, RET_HEAD_DIM), BF16)],
        compiler_params=pltpu.CompilerParams(
            dimension_semantics=("parallel", "parallel"), vmem_limit_bytes=VMEM_LIMIT_BYTES),
        name="retention",
    )(rq.reshape(B, S, RET_WIDTH), rkt, rv.reshape(B, S, RET_WIDTH), gate.reshape(B, S, RET_WIDTH), lf, lb)


def _attention_kernel(q_ref, k_ref, vt_ref, o_ref, acc_ref, s_ref, mb_ref, *, n_q, n_kv, tq, tk):
    def produce(qi, j, slot):
        qoff = pl.multiple_of(qi * tq, tq)
        koff = pl.multiple_of(j * tk, tk)
        s = lax.dot_general(k_ref[0, pl.ds(koff, tk), :], q_ref[0, pl.ds(qoff, tq), :],
                            (((1,), (1,)), ((), ())), preferred_element_type=F32)
        s_ref[slot] = s
        mb_ref[slot] = jnp.max(s, axis=0, keepdims=True)

    def consume(j, slot, m):
        koff = pl.multiple_of(j * tk, tk)
        m_new = jnp.maximum(m, mb_ref[slot])
        alpha = jnp.exp2(m - m_new)
        p = jnp.exp2(s_ref[slot] - m_new).astype(BF16)
        pv = jnp.dot(vt_ref[0, :, pl.ds(koff, tk)], p, preferred_element_type=F32)
        acc_ref[...] = alpha * acc_ref[...] + pv
        return m_new

    def query_block(qi, carry, last_q):
        def group(g, m, last):
            j0 = g * KV_UNROLL
            for i in range(KV_UNROLL):
                if not (last and i == KV_UNROLL - 1):
                    produce(qi, j0 + i + 1, (i + 1) % 2)
                elif not last_q:
                    produce(qi + 1, 0, 0)
                m = consume(j0 + i, i % 2, m)
            return m

        n_groups = n_kv // KV_UNROLL
        acc_ref[...] = jnp.zeros_like(acc_ref)
        m = jnp.full((1, tq), -jnp.inf, F32)
        if n_groups > 1:
            m = lax.fori_loop(0, n_groups - 1, functools.partial(group, last=False), m)
        group(n_groups - 1, m, last=True)
        qoff = pl.multiple_of(qi * tq, tq)
        acc = acc_ref[...]
        o = acc[0:MLA_V_DIM, :] / acc[MLA_V_DIM:MLA_V_DIM + 1, :]
        o_ref[0, pl.ds(qoff, tq), :] = o.T.astype(BF16)
        return carry

    produce(0, 0, 0)
    if n_q > 1:
        lax.fori_loop(0, n_q - 1, functools.partial(query_block, last_q=False), 0)
    query_block(n_q - 1, 0, last_q=True)


def _attention(qcat, kcat, vt, B, S):
    tq = TQ_ATT
    tk = min(TK_ATT, S // (2 * KV_UNROLL))
    assert S % tq == 0 and S % tk == 0 and KV_UNROLL % 2 == 0 and (S // tk) % KV_UNROLL == 0, (S, tq, tk)
    qk_spec = pl.BlockSpec((1, S, MLA_QK_DIM), lambda b, h: (b, 0, h))
    return pl.pallas_call(
        functools.partial(_attention_kernel, n_q=S // tq, n_kv=S // tk, tq=tq, tk=tk),
        grid=(B, MLA_HEADS),
        in_specs=[qk_spec, qk_spec, pl.BlockSpec((1, V_EXT_ROWS, S), lambda b, h: (b, h, 0))],
        out_specs=pl.BlockSpec((1, S, MLA_V_DIM), lambda b, h: (b, 0, h)),
        out_shape=jax.ShapeDtypeStruct((B, S, MLA_WIDTH), BF16),
        scratch_shapes=[pltpu.VMEM((V_EXT_ROWS, tq), F32), pltpu.VMEM((2, tk, tq), F32),
                        pltpu.VMEM((2, 1, tq), F32)],
        compiler_params=pltpu.CompilerParams(
            dimension_semantics=("parallel", "parallel"), vmem_limit_bytes=VMEM_LIMIT_BYTES),
        name="mla_attention",
    )(qcat.reshape(B, S, MLA_HEADS * MLA_QK_DIM), kcat.reshape(B, S, MLA_HEADS * MLA_QK_DIM), vt)


def _out_proj_kernel(ret_ref, mla_ref, x_ref, w_ref, g_ref, o_ref):
    mix = jnp.dot(ret_ref[...], w_ref[0:RET_WIDTH, :], preferred_element_type=F32)
    mix = mix + jnp.dot(mla_ref[...], w_ref[RET_WIDTH:, :], preferred_element_type=F32)
    o_ref[...] = x_ref[...] + _rms(mix, g_ref[...])


def _out_proj(o_ret, o_mla, x2, p):
    T = x2.shape[0]
    tm = TM_OUT
    assert T % tm == 0, (T, tm)
    row = lambda i: (i, 0)
    return pl.pallas_call(
        _out_proj_kernel,
        grid=(T // tm,),
        in_specs=[
            pl.BlockSpec((tm, RET_WIDTH), row),
            pl.BlockSpec((tm, MLA_WIDTH), row),
            pl.BlockSpec((tm, D_MODEL), row),
            _const_spec((RET_WIDTH + MLA_WIDTH, D_MODEL)),
            _const_spec((1, D_MODEL)),
        ],
        out_specs=pl.BlockSpec((tm, D_MODEL), row),
        out_shape=jax.ShapeDtypeStruct((T, D_MODEL), F32),
        compiler_params=pltpu.CompilerParams(
            dimension_semantics=("parallel",), vmem_limit_bytes=VMEM_LIMIT_BYTES),
        name="out_proj",
    )(o_ret.reshape(T, RET_WIDTH), o_mla.reshape(T, MLA_WIDTH), x2, p["w_out"], p["g_mix_post"])


def _ffn_kernel(x_ref, gpre_ref, wup_ref, wdn_ref, gpost_ref, o_ref, h_ref):
    j = pl.program_id(1)

    n_chunks = x_ref.shape[0] // NORM_ROWS

    @pl.when(j == 0)
    def _():
        def pre(r, carry):
            rows = pl.ds(pl.multiple_of(r * NORM_ROWS, NORM_ROWS), NORM_ROWS)
            h_ref[rows, :] = _rms(x_ref[rows, :], gpre_ref[...]).astype(BF16)
            o_ref[rows, :] = jnp.zeros((NORM_ROWS, D_MODEL), F32)
            return carry
        lax.fori_loop(0, n_chunks, pre, 0, unroll=NORM_UNROLL)

    a = jnp.maximum(jnp.dot(h_ref[...], wup_ref[...], preferred_element_type=F32), 0.0)
    o_ref[...] += jnp.dot((a * a).astype(BF16), wdn_ref[...], preferred_element_type=F32)

    @pl.when(j == pl.num_programs(1) - 1)
    def _():
        for r in range(n_chunks):
            rows = slice(r * NORM_ROWS, (r + 1) * NORM_ROWS)
            o_ref[rows, :] = x_ref[rows, :] + _rms(o_ref[rows, :], gpost_ref[...])


def _ffn(x1, p):
    T = x1.shape[0]
    tm, tf = TM_FFN, TF_FFN
    assert T % tm == 0 and FFN_DIM % tf == 0 and tm % NORM_ROWS == 0, (T, tm, tf)
    return pl.pallas_call(
        _ffn_kernel,
        grid=(T // tm, FFN_DIM // tf),
        in_specs=[
            pl.BlockSpec((tm, D_MODEL), lambda i, j: (i, 0)),
            _const_spec((1, D_MODEL)),
            pl.BlockSpec((D_MODEL, tf), lambda i, j: (0, j)),
            pl.BlockSpec((tf, D_MODEL), lambda i, j: (j, 0)),
            _const_spec((1, D_MODEL)),
        ],
        out_specs=pl.BlockSpec((tm, D_MODEL), lambda i, j: (i, 0)),
        out_shape=jax.ShapeDtypeStruct((T, D_MODEL), F32),
        scratch_shapes=[pltpu.VMEM((tm, D_MODEL), BF16)],
        compiler_params=pltpu.CompilerParams(
            dimension_semantics=("parallel", "arbitrary"), vmem_limit_bytes=VMEM_LIMIT_BYTES),
        name="ffn",
    )(x1, p["g_ffn_pre"], p["w_ffn_up"], p["w_ffn_down"], p["g_ffn_post"])


def _rope_tables(S):
    lo_n = 64

    def cos_sin(d):
        inv = (ROPE_BASE ** (-jnp.arange(0, d, 2, dtype=F32) / d))[None, :]
        a_hi = jnp.arange(0, S, lo_n, dtype=F32)[:, None] * inv
        a_lo = jnp.arange(lo_n, dtype=F32)[:, None] * inv
        ch, sh = jnp.cos(a_hi)[:, None, :], jnp.sin(a_hi)[:, None, :]
        cl, sl = jnp.cos(a_lo)[None, :, :], jnp.sin(a_lo)[None, :, :]
        return (ch * cl - sh * sl).reshape(S, d // 2), (sh * cl + ch * sl).reshape(S, d // 2)

    c_r, s_r = cos_sin(RET_HEAD_DIM)
    c_m, s_m = cos_sin(MLA_ROPE_DIM)
    cos_r = jnp.concatenate([c_r] * 2, axis=1)
    sin_r = jnp.concatenate([-s_r, s_r], axis=1)
    cos_m = jnp.concatenate([c_m] * 4, axis=1)
    sin_m = jnp.concatenate([-s_m] * 2 + [s_m] * 2, axis=1)
    return cos_r, sin_r, cos_m, sin_m


def _layout_params(norm_mix_pre, w_in, ret_decay_fwd, ret_decay_bwd, mla_q_norm, w_q_up, mla_kv_norm, w_kv_up,
                   w_out, norm_mix_post, norm_ffn_pre, w_ffn_up, w_ffn_down, norm_ffn_post):
    half = MLA_ROPE_DIM // 2
    main = 4 * RET_WIDTH + MLA_Q_RANK + MLA_KV_RANK
    kr1, kr2 = w_in[:, main: main + half], w_in[:, main + half:]
    w_kr = jnp.concatenate([kr1, kr1, kr2, kr2], axis=1).astype(BF16)

    per_head = MLA_NOPE_DIM + MLA_ROPE_DIM
    wq = w_q_up.reshape(MLA_Q_RANK, MLA_HEADS, per_head)
    nope = wq[:, :, :MLA_NOPE_DIM].reshape(MLA_Q_RANK, MLA_HEADS * MLA_NOPE_DIM)
    x1 = wq[:, :, MLA_NOPE_DIM: MLA_NOPE_DIM + half].reshape(MLA_Q_RANK, MLA_HEADS // 2, 2 * half)
    x2 = wq[:, :, MLA_NOPE_DIM + half:].reshape(MLA_Q_RANK, MLA_HEADS // 2, 2 * half)
    slabs = jnp.concatenate([x1, x2], axis=2).reshape(MLA_Q_RANK, MLA_HEADS * MLA_ROPE_DIM)
    w_q_p = jnp.concatenate([nope, slabs], axis=1).astype(BF16)

    wkv = w_kv_up.reshape(MLA_KV_RANK, MLA_HEADS, MLA_NOPE_DIM + MLA_V_DIM)
    w_kv_p = jnp.concatenate([
        wkv[:, :, :MLA_NOPE_DIM].reshape(MLA_KV_RANK, MLA_HEADS * MLA_NOPE_DIM),
        wkv[:, :, MLA_NOPE_DIM:].reshape(MLA_KV_RANK, MLA_HEADS * MLA_V_DIM)], axis=1).astype(BF16)

    bcast = lambda v: jnp.broadcast_to(v.astype(F32)[:, None, None], (RET_HEADS, 1, LANES))
    return {
        "g_mix_pre": norm_mix_pre.reshape(1, D_MODEL), "w_in": w_in.astype(BF16), "w_kr": w_kr,
        "lf": bcast(ret_decay_fwd), "lb": bcast(ret_decay_bwd),
        "g_q": mla_q_norm.reshape(1, MLA_Q_RANK), "w_q_up": w_q_p,
        "g_kv": mla_kv_norm.reshape(1, MLA_KV_RANK), "w_kv_up": w_kv_p,
        "w_out": w_out.astype(BF16), "g_mix_post": norm_mix_post.reshape(1, D_MODEL),
        "g_ffn_pre": norm_ffn_pre.reshape(1, D_MODEL), "w_ffn_up": w_ffn_up.astype(BF16),
        "w_ffn_down": w_ffn_down.astype(BF16), "g_ffn_post": norm_ffn_post.reshape(1, D_MODEL),
    }


def _layer(x, p):
    B, S, _ = x.shape
    x2 = x.reshape(B * S, D_MODEL)
    rq, rkt, rv, gate, qcat, kcat, vt = _in_proj(x2, B, S, p)
    o_ret = _retention(rq, rkt, rv, gate, p["lf"], p["lb"], B, S)
    o_mla = _attention(qcat, kcat, vt, B, S)
    x1 = _out_proj(o_ret, o_mla, x2, p)
    return _ffn(x1, p).reshape(B, S, D_MODEL)


def kernel(x_prompt, x_sample, norm_mix_pre, w_in, ret_decay_fwd, ret_decay_bwd, mla_q_norm, w_q_up, mla_kv_norm,
           w_kv_up, w_out, norm_mix_post, norm_ffn_pre, w_ffn_up, w_ffn_down, norm_ffn_post):
    y_prompt, y_sample = x_prompt, x_sample
    tables = dict(zip(("cos_r", "sin_r", "cos_m", "sin_m"), _rope_tables(max(x_prompt.shape[1], x_sample.shape[1]))))
    for l in range(norm_mix_pre.shape[0]):
        p = _layout_params(norm_mix_pre[l], w_in[l], ret_decay_fwd[l], ret_decay_bwd[l], mla_q_norm[l], w_q_up[l],
                           mla_kv_norm[l], w_kv_up[l], w_out[l], norm_mix_post[l], norm_ffn_pre[l], w_ffn_up[l],
                           w_ffn_down[l], norm_ffn_post[l])
        p.update(tables)
        y_prompt = _layer(y_prompt, p)
        y_sample = _layer(y_sample, p)
    return (y_prompt, y_sample)
```

```python
import functools
import math

import jax
import jax.numpy as jnp
from jax import lax
from jax.experimental import pallas as pl
from jax.experimental.pallas import tpu as pltpu

D_MODEL = 2048
RET_HEADS = 8
RET_HEAD_DIM = 128
RET_WIDTH = RET_HEADS * RET_HEAD_DIM
RET_CHUNK = 128
MLA_HEADS = 8
MLA_NOPE_DIM = 128
MLA_ROPE_DIM = 64
MLA_V_DIM = 128
MLA_Q_RANK = 512
MLA_KV_RANK = 512
MLA_WIDTH = MLA_HEADS * MLA_V_DIM
FFN_DIM = 4 * D_MODEL
ROPE_BASE = 10000.0
NORM_EPS = 1e-6

LANES = 128
MLA_QK_DIM = 2 * LANES
BF16_SUBLANES = 16
V_EXT_ROWS = MLA_V_DIM + BF16_SUBLANES
IN_COLS = 4 * RET_WIDTH + MLA_Q_RANK + MLA_KV_RANK + MLA_ROPE_DIM
VMEM_LIMIT_BYTES = 60000 * 1024

F32 = jnp.float32
Q_SCALE_LOG2 = (MLA_NOPE_DIM + MLA_ROPE_DIM) ** -0.5 * math.log2(math.e)
BF16 = jnp.bfloat16

TM_IN = 256
TM_OUT = 512
TM_FFN = 1024
TF_FFN = 1024
NORM_ROWS = 16
NORM_UNROLL = 8
TQ_ATT = 1024
TK_ATT = 2048
RET_BLOCK = 256
RET_UNROLL = 8
KV_UNROLL = 2


def _const_spec(shape):
    return pl.BlockSpec(shape, lambda *_: (0,) * len(shape), pipeline_mode=pl.Buffered(1))


def _rms(x, gain):
    return x * lax.rsqrt(jnp.mean(x * x, axis=-1, keepdims=True) + NORM_EPS) * gain


def _rope_slab(x, cos, sin_signed):
    return x * cos + pltpu.roll(x, LANES // 2, 1) * sin_signed


def _in_proj_kernel(x_ref, g_ref, w_ref, wkr_ref, qg_ref, kvg_ref, wq_ref, wkv_ref, cr_ref, sr_ref, cm_ref, sm_ref,
                    rq_ref, rkt_ref, rv_ref, gate_ref, qcat_ref, kcat_ref, vt_ref):
    x = x_ref[...]
    h = (x * g_ref[...]).astype(BF16)
    r = lax.rsqrt(jnp.mean(x * x, axis=-1, keepdims=True) + NORM_EPS)

    def proj(lo, hi):
        return jnp.dot(h, w_ref[:, lo:hi], preferred_element_type=F32) * r

    cr, sr = cr_ref[...], sr_ref[...]
    cm, sm = cm_ref[...], sm_ref[...]

    off = 4 * RET_WIDTH
    cq = _rms(proj(off, off + MLA_Q_RANK), qg_ref[...]).astype(BF16)
    qf = jnp.dot(cq, wq_ref[...], preferred_element_type=F32) * Q_SCALE_LOG2
    lane = lax.broadcasted_iota(jnp.int32, (1, LANES), 1)
    even_lanes = (lane // (MLA_ROPE_DIM // 2)) % 2 == 0
    nope_w = MLA_HEADS * MLA_NOPE_DIM
    for j in range(MLA_HEADS // 2):
        slab = _rope_slab(qf[:, nope_w + j * LANES: nope_w + (j + 1) * LANES], cm, sm)
        for par in range(2):
            hh = 2 * j + par
            base = hh * MLA_QK_DIM
            qcat_ref[:, base: base + LANES] = qf[:, hh * LANES: (hh + 1) * LANES].astype(BF16)
            keep = even_lanes if par == 0 else jnp.logical_not(even_lanes)
            qcat_ref[:, base + LANES: base + 2 * LANES] = jnp.where(keep, slab, 0.0).astype(BF16)

    off += MLA_Q_RANK
    ckv = _rms(proj(off, off + MLA_KV_RANK), kvg_ref[...]).astype(BF16)
    kvf = jnp.dot(ckv, wkv_ref[...], preferred_element_type=F32)
    kext = _rope_slab(jnp.dot(h, wkr_ref[...], preferred_element_type=F32) * r, cm, sm).astype(BF16)
    for hh in range(MLA_HEADS):
        base = hh * MLA_QK_DIM
        kcat_ref[:, base: base + LANES] = kvf[:, hh * LANES: (hh + 1) * LANES].astype(BF16)
        kcat_ref[:, base + LANES: base + 2 * LANES] = kext
    for hh in range(MLA_HEADS):
        sl = slice(nope_w + hh * LANES, nope_w + (hh + 1) * LANES)
        vt_ref[0, hh * V_EXT_ROWS: hh * V_EXT_ROWS + MLA_V_DIM, :] = kvf[:, sl].T.astype(BF16)
        vt_ref[0, hh * V_EXT_ROWS + MLA_V_DIM: (hh + 1) * V_EXT_ROWS, :] = jnp.ones(
            (BF16_SUBLANES, x_ref.shape[0]), BF16)

    q = proj(0, RET_WIDTH)
    for hh in range(RET_HEADS):
        sl = slice(hh * LANES, (hh + 1) * LANES)
        rq_ref[:, sl] = _rope_slab(q[:, sl], cr, sr).astype(BF16)

    k = proj(RET_WIDTH, 2 * RET_WIDTH)
    k_scale = RET_HEAD_DIM ** -0.5
    for hh in range(RET_HEADS):
        sl = slice(hh * LANES, (hh + 1) * LANES)
        rkt_ref[0, sl, :] = (_rope_slab(k[:, sl], cr, sr) * k_scale).T.astype(BF16)

    rv_ref[...] = proj(2 * RET_WIDTH, 3 * RET_WIDTH).astype(BF16)

    g = proj(3 * RET_WIDTH, 4 * RET_WIDTH)
    gate_ref[...] = (g / (1.0 + jnp.exp(-g))).astype(BF16)


def _in_proj(x2, B, S, p):
    T = B * S
    tm = TM_IN
    assert S % tm == 0, (S, tm)
    ns = S // tm
    row = lambda i: (i, 0)
    pos = lambda i: (i % ns, 0)
    tpose = lambda i: (i // ns, 0, i % ns)
    bf = lambda n: jax.ShapeDtypeStruct((T, n), BF16)
    bft = jax.ShapeDtypeStruct((B, RET_WIDTH, S), BF16)
    bfv = jax.ShapeDtypeStruct((B, MLA_HEADS * V_EXT_ROWS, S), BF16)
    return pl.pallas_call(
        _in_proj_kernel,
        grid=(T // tm,),
        in_specs=[
            pl.BlockSpec((tm, D_MODEL), row),
            _const_spec((1, D_MODEL)),
            _const_spec((D_MODEL, IN_COLS)),
            _const_spec((D_MODEL, LANES)),
            _const_spec((1, MLA_Q_RANK)),
            _const_spec((1, MLA_KV_RANK)),
            _const_spec((MLA_Q_RANK, MLA_HEADS * (MLA_NOPE_DIM + MLA_ROPE_DIM))),
            _const_spec((MLA_KV_RANK, MLA_HEADS * (MLA_NOPE_DIM + MLA_V_DIM))),
            pl.BlockSpec((tm, LANES), pos),
            pl.BlockSpec((tm, LANES), pos),
            pl.BlockSpec((tm, LANES), pos),
            pl.BlockSpec((tm, LANES), pos),
        ],
        out_specs=[
            pl.BlockSpec((tm, RET_WIDTH), row),
            pl.BlockSpec((1, RET_WIDTH, tm), tpose),
            pl.BlockSpec((tm, RET_WIDTH), row),
            pl.BlockSpec((tm, RET_WIDTH), row),
            pl.BlockSpec((tm, MLA_HEADS * MLA_QK_DIM), row),
            pl.BlockSpec((tm, MLA_HEADS * MLA_QK_DIM), row),
            pl.BlockSpec((1, MLA_HEADS * V_EXT_ROWS, tm), tpose),
        ],
        out_shape=[bf(RET_WIDTH), bft, bf(RET_WIDTH), bf(RET_WIDTH),
                   bf(MLA_HEADS * MLA_QK_DIM), bf(MLA_HEADS * MLA_QK_DIM), bfv],
        compiler_params=pltpu.CompilerParams(
            dimension_semantics=("parallel",), vmem_limit_bytes=VMEM_LIMIT_BYTES),
        name="in_proj",
    )(x2, p["g_mix_pre"], p["w_in"], p["w_kr"], p["g_q"], p["g_kv"], p["w_q_up"], p["w_kv_up"],
      p["cos_r"], p["sin_r"], p["cos_m"], p["sin_m"])


def _log_sigmoid(x):
    return jnp.minimum(x, 0.0) - jnp.log(1.0 + jnp.exp(-jnp.abs(x)))


def _retention_kernel(q_ref, kt_ref, v_ref, g_ref, lf_ref, lb_ref, o_ref, st_ref, *, n_chunks):
    C, D = RET_BLOCK, RET_HEAD_DIM
    lf = _log_sigmoid(lf_ref[0])
    lb = _log_sigmoid(lb_ref[0])
    lf1, lb1 = lf[:, 0:1], lb[:, 0:1]

    def iota(shape, dim):
        return lax.broadcasted_iota(jnp.int32, shape, dim).astype(F32)

    diff = iota((C, C), 0) - iota((C, C), 1)
    decay = jnp.exp(jnp.where(diff >= 0, lf1 * diff, -lb1 * diff))
    pos_t = iota((D, C), 1)
    wkf_t = jnp.exp(lf1 * (C - 1.0 - pos_t))
    wkb_t = jnp.exp(lb1 * pos_t)
    pos = iota((C, D), 0)
    wqf = jnp.exp(lf * (pos + 1.0))
    wqb = jnp.exp(lb * (C - pos))
    df = jnp.exp(lf * C)
    db = jnp.exp(lb * C)

    def chunk_kv(c, w_t):
        off = pl.multiple_of(c * C, C)
        kt = kt_ref[0, :, pl.ds(off, C)].astype(F32)
        return jnp.dot((kt * w_t).astype(BF16), v_ref[0, pl.ds(off, C), :], preferred_element_type=F32)

    def scan_states(t, states):
        sf, sb = states
        cb = n_chunks - 1 - t
        st_ref[t, 0:D, :] = sf.astype(BF16)
        st_ref[cb, D:2 * D, :] = sb.astype(BF16)
        return sf * df + chunk_kv(t, wkf_t), sb * db + chunk_kv(cb, wkb_t)

    zero = jnp.zeros((D, D), F32)
    lax.fori_loop(0, n_chunks, scan_states, (zero, zero), unroll=RET_UNROLL)

    def out_chunk(c, carry):
        off = pl.multiple_of(c * C, C)
        q = q_ref[0, pl.ds(off, C), :]
        v = v_ref[0, pl.ds(off, C), :]
        s = jnp.dot(q, kt_ref[0, :, pl.ds(off, C)], preferred_element_type=F32)
        o = jnp.dot((s * decay).astype(BF16), v, preferred_element_type=F32)
        qf = q.astype(F32)
        qq = jnp.concatenate([(qf * wqf).astype(BF16), (qf * wqb).astype(BF16)], axis=1)
        o = o + jnp.dot(qq, st_ref[c], preferred_element_type=F32)
        o = o * lax.rsqrt(jnp.mean(o * o, axis=-1, keepdims=True) + NORM_EPS)
        o_ref[0, pl.ds(off, C), :] = (o * g_ref[0, pl.ds(off, C), :].astype(F32)).astype(BF16)
        return carry

    lax.fori_loop(0, n_chunks, out_chunk, 0, unroll=RET_UNROLL)


def _retention(rq, rkt, rv, gate, lf, lb, B, S):
    assert S % RET_BLOCK == 0, (S, RET_BLOCK)
    n = S // RET_BLOCK
    tok = pl.BlockSpec((1, S, LANES), lambda b, h: (b, 0, h))
    tok_t = pl.BlockSpec((1, LANES, S), lambda b, h: (b, h, 0))
    dec = pl.BlockSpec((1, 1, LANES), lambda b, h: (h, 0, 0))
    return pl.pallas_call(
        functools.partial(_retention_kernel, n_chunks=n),
        grid=(B, RET_HEADS),
        in_specs=[tok, tok_t, tok, tok, dec, dec],
        out_specs=tok,
        out_shape=jax.ShapeDtypeStruct((B, S, RET_WIDTH), BF16),
        scratch_shapes=[pltpu.VMEM((n, 2 * RET_HEAD_DIM, RET_HEAD_DIM), BF16)],
        compiler_params=pltpu.CompilerParams(
            dimension_semantics=("parallel", "parallel"), vmem_limit_bytes=VMEM_LIMIT_BYTES),
        name="retention",
    )(rq.reshape(B, S, RET_WIDTH), rkt, rv.reshape(B, S, RET_WIDTH), gate.reshape(B, S, RET_WIDTH), lf, lb)


def _attention_kernel(q_ref, k_ref, vt_ref, o_ref, acc_ref, s_ref, mb_ref, *, n_q, n_kv, tq, tk):
    def produce(qi, j, slot):
        qoff = pl.multiple_of(qi * tq, tq)
        koff = pl.multiple_of(j * tk, tk)
        s = lax.dot_general(k_ref[0, pl.ds(koff, tk), :], q_ref[0, pl.ds(qoff, tq), :],
                            (((1,), (1,)), ((), ())), preferred_element_type=F32)
        s_ref[slot] = s
        mb_ref[slot] = jnp.max(s, axis=0, keepdims=True)

    def consume(j, slot, m):
        koff = pl.multiple_of(j * tk, tk)
        m_new = jnp.maximum(m, mb_ref[slot])
        alpha = jnp.exp2(m - m_new)
        p = jnp.exp2(s_ref[slot] - m_new).astype(BF16)
        pv = jnp.dot(vt_ref[0, :, pl.ds(koff, tk)], p, preferred_element_type=F32)
        acc_ref[...] = alpha * acc_ref[...] + pv
        return m_new

    def query_block(qi, carry, last_q):
        def group(g, m, last):
            j0 = g * KV_UNROLL
            for i in range(KV_UNROLL):
                if not (last and i == KV_UNROLL - 1):
                    produce(qi, j0 + i + 1, (i + 1) % 2)
                elif not last_q:
                    produce(qi + 1, 0, 0)
                m = consume(j0 + i, i % 2, m)
            return m

        n_groups = n_kv // KV_UNROLL
        acc_ref[...] = jnp.zeros_like(acc_ref)
        m = jnp.full((1, tq), -jnp.inf, F32)
        if n_groups > 1:
            m = lax.fori_loop(0, n_groups - 1, functools.partial(group, last=False), m)
        group(n_groups - 1, m, last=True)
        qoff = pl.multiple_of(qi * tq, tq)
        acc = acc_ref[...]
        o = acc[0:MLA_V_DIM, :] / acc[MLA_V_DIM:MLA_V_DIM + 1, :]
        o_ref[0, pl.ds(qoff, tq), :] = o.T.astype(BF16)
        return carry

    produce(0, 0, 0)
    if n_q > 1:
        lax.fori_loop(0, n_q - 1, functools.partial(query_block, last_q=False), 0)
    query_block(n_q - 1, 0, last_q=True)


def _attention(qcat, kcat, vt, B, S):
    tq = TQ_ATT
    tk = min(TK_ATT, S // KV_UNROLL)
    assert S % tq == 0 and S % tk == 0 and KV_UNROLL % 2 == 0 and (S // tk) % KV_UNROLL == 0, (S, tq, tk)
    qk_spec = pl.BlockSpec((1, S, MLA_QK_DIM), lambda b, h: (b, 0, h))
    return pl.pallas_call(
        functools.partial(_attention_kernel, n_q=S // tq, n_kv=S // tk, tq=tq, tk=tk),
        grid=(B, MLA_HEADS),
        in_specs=[qk_spec, qk_spec, pl.BlockSpec((1, V_EXT_ROWS, S), lambda b, h: (b, h, 0))],
        out_specs=pl.BlockSpec((1, S, MLA_V_DIM), lambda b, h: (b, 0, h)),
        out_shape=jax.ShapeDtypeStruct((B, S, MLA_WIDTH), BF16),
        scratch_shapes=[pltpu.VMEM((V_EXT_ROWS, tq), F32), pltpu.VMEM((2, tk, tq), F32),
                        pltpu.VMEM((2, 1, tq), F32)],
        compiler_params=pltpu.CompilerParams(
            dimension_semantics=("parallel", "parallel"), vmem_limit_bytes=VMEM_LIMIT_BYTES),
        name="mla_attention",
    )(qcat.reshape(B, S, MLA_HEADS * MLA_QK_DIM), kcat.reshape(B, S, MLA_HEADS * MLA_QK_DIM), vt)


def _out_proj_kernel(ret_ref, mla_ref, x_ref, w_ref, g_ref, o_ref):
    mix = jnp.dot(ret_ref[...], w_ref[0:RET_WIDTH, :], preferred_element_type=F32)
    mix = mix + jnp.dot(mla_ref[...], w_ref[RET_WIDTH:, :], preferred_element_type=F32)
    o_ref[...] = x_ref[...] + _rms(mix, g_ref[...])


def _out_proj(o_ret, o_mla, x2, p):
    T = x2.shape[0]
    tm = TM_OUT
    assert T % tm == 0, (T, tm)
    row = lambda i: (i, 0)
    return pl.pallas_call(
        _out_proj_kernel,
        grid=(T // tm,),
        in_specs=[
            pl.BlockSpec((tm, RET_WIDTH), row),
            pl.BlockSpec((tm, MLA_WIDTH), row),
            pl.BlockSpec((tm, D_MODEL), row),
            _const_spec((RET_WIDTH + MLA_WIDTH, D_MODEL)),
            _const_spec((1, D_MODEL)),
        ],
        out_specs=pl.BlockSpec((tm, D_MODEL), row),
        out_shape=jax.ShapeDtypeStruct((T, D_MODEL), F32),
        compiler_params=pltpu.CompilerParams(
            dimension_semantics=("parallel",), vmem_limit_bytes=VMEM_LIMIT_BYTES),
        name="out_proj",
    )(o_ret.reshape(T, RET_WIDTH), o_mla.reshape(T, MLA_WIDTH), x2, p["w_out"], p["g_mix_post"])


def _ffn_kernel(x_ref, gpre_ref, wup_ref, wdn_ref, gpost_ref, o_ref, h_ref):
    j = pl.program_id(1)

    n_chunks = x_ref.shape[0] // NORM_ROWS

    @pl.when(j == 0)
    def _():
        def pre(r, carry):
            rows = pl.ds(pl.multiple_of(r * NORM_ROWS, NORM_ROWS), NORM_ROWS)
            h_ref[rows, :] = _rms(x_ref[rows, :], gpre_ref[...]).astype(BF16)
            o_ref[rows, :] = jnp.zeros((NORM_ROWS, D_MODEL), F32)
            return carry
        lax.fori_loop(0, n_chunks, pre, 0, unroll=NORM_UNROLL)

    a = jnp.maximum(jnp.dot(h_ref[...], wup_ref[...], preferred_element_type=F32), 0.0)
    o_ref[...] += jnp.dot((a * a).astype(BF16), wdn_ref[...], preferred_element_type=F32)

    @pl.when(j == pl.num_programs(1) - 1)
    def _():
        for r in range(n_chunks):
            rows = slice(r * NORM_ROWS, (r + 1) * NORM_ROWS)
            o_ref[rows, :] = x_ref[rows, :] + _rms(o_ref[rows, :], gpost_ref[...])


def _ffn(x1, p):
    T = x1.shape[0]
    tm, tf = TM_FFN, TF_FFN
    assert T % tm == 0 and FFN_DIM % tf == 0 and tm % NORM_ROWS == 0, (T, tm, tf)
    return pl.pallas_call(
        _ffn_kernel,
        grid=(T // tm, FFN_DIM // tf),
        in_specs=[
            pl.BlockSpec((tm, D_MODEL), lambda i, j: (i, 0)),
            _const_spec((1, D_MODEL)),
            pl.BlockSpec((D_MODEL, tf), lambda i, j: (0, j)),
            pl.BlockSpec((tf, D_MODEL), lambda i, j: (j, 0)),
            _const_spec((1, D_MODEL)),
        ],
        out_specs=pl.BlockSpec((tm, D_MODEL), lambda i, j: (i, 0)),
        out_shape=jax.ShapeDtypeStruct((T, D_MODEL), F32),
        scratch_shapes=[pltpu.VMEM((tm, D_MODEL), BF16)],
        compiler_params=pltpu.CompilerParams(
            dimension_semantics=("parallel", "arbitrary"), vmem_limit_bytes=VMEM_LIMIT_BYTES),
        name="ffn",
    )(x1, p["g_ffn_pre"], p["w_ffn_up"], p["w_ffn_down"], p["g_ffn_post"])


def _rope_tables(S):
    lo_n = 64

    def cos_sin(d):
        inv = (ROPE_BASE ** (-jnp.arange(0, d, 2, dtype=F32) / d))[None, :]
        a_hi = jnp.arange(0, S, lo_n, dtype=F32)[:, None] * inv
        a_lo = jnp.arange(lo_n, dtype=F32)[:, None] * inv
        ch, sh = jnp.cos(a_hi)[:, None, :], jnp.sin(a_hi)[:, None, :]
        cl, sl = jnp.cos(a_lo)[None, :, :], jnp.sin(a_lo)[None, :, :]
        return (ch * cl - sh * sl).reshape(S, d // 2), (sh * cl + ch * sl).reshape(S, d // 2)

    c_r, s_r = cos_sin(RET_HEAD_DIM)
    c_m, s_m = cos_sin(MLA_ROPE_DIM)
    cos_r = jnp.concatenate([c_r] * 2, axis=1)
    sin_r = jnp.concatenate([-s_r, s_r], axis=1)
    cos_m = jnp.concatenate([c_m] * 4, axis=1)
    sin_m = jnp.concatenate([-s_m] * 2 + [s_m] * 2, axis=1)
    return cos_r, sin_r, cos_m, sin_m


def _layout_params(norm_mix_pre, w_in, ret_decay_fwd, ret_decay_bwd, mla_q_norm, w_q_up, mla_kv_norm, w_kv_up,
                   w_out, norm_mix_post, norm_ffn_pre, w_ffn_up, w_ffn_down, norm_ffn_post):
    half = MLA_ROPE_DIM // 2
    main = 4 * RET_WIDTH + MLA_Q_RANK + MLA_KV_RANK
    kr1, kr2 = w_in[:, main: main + half], w_in[:, main + half:]
    w_kr = jnp.concatenate([kr1, kr1, kr2, kr2], axis=1).astype(BF16)

    per_head = MLA_NOPE_DIM + MLA_ROPE_DIM
    wq = w_q_up.reshape(MLA_Q_RANK, MLA_HEADS, per_head)
    nope = wq[:, :, :MLA_NOPE_DIM].reshape(MLA_Q_RANK, MLA_HEADS * MLA_NOPE_DIM)
    x1 = wq[:, :, MLA_NOPE_DIM: MLA_NOPE_DIM + half].reshape(MLA_Q_RANK, MLA_HEADS // 2, 2 * half)
    x2 = wq[:, :, MLA_NOPE_DIM + half:].reshape(MLA_Q_RANK, MLA_HEADS // 2, 2 * half)
    slabs = jnp.concatenate([x1, x2], axis=2).reshape(MLA_Q_RANK, MLA_HEADS * MLA_ROPE_DIM)
    w_q_p = jnp.concatenate([nope, slabs], axis=1).astype(BF16)

    wkv = w_kv_up.reshape(MLA_KV_RANK, MLA_HEADS, MLA_NOPE_DIM + MLA_V_DIM)
    w_kv_p = jnp.concatenate([
        wkv[:, :, :MLA_NOPE_DIM].reshape(MLA_KV_RANK, MLA_HEADS * MLA_NOPE_DIM),
        wkv[:, :, MLA_NOPE_DIM:].reshape(MLA_KV_RANK, MLA_HEADS * MLA_V_DIM)], axis=1).astype(BF16)

    bcast = lambda v: jnp.broadcast_to(v.astype(F32)[:, None, None], (RET_HEADS, 1, LANES))
    return {
        "g_mix_pre": norm_mix_pre.reshape(1, D_MODEL), "w_in": w_in.astype(BF16), "w_kr": w_kr,
        "lf": bcast(ret_decay_fwd), "lb": bcast(ret_decay_bwd),
        "g_q": mla_q_norm.reshape(1, MLA_Q_RANK), "w_q_up": w_q_p,
        "g_kv": mla_kv_norm.reshape(1, MLA_KV_RANK), "w_kv_up": w_kv_p,
        "w_out": w_out.astype(BF16), "g_mix_post": norm_mix_post.reshape(1, D_MODEL),
        "g_ffn_pre": norm_ffn_pre.reshape(1, D_MODEL), "w_ffn_up": w_ffn_up.astype(BF16),
        "w_ffn_down": w_ffn_down.astype(BF16), "g_ffn_post": norm_ffn_post.reshape(1, D_MODEL),
    }


def _layer(x, p):
    B, S, _ = x.shape
    x2 = x.reshape(B * S, D_MODEL)
    rq, rkt, rv, gate, qcat, kcat, vt = _in_proj(x2, B, S, p)
    o_ret = _retention(rq, rkt, rv, gate, p["lf"], p["lb"], B, S)
    o_mla = _attention(qcat, kcat, vt, B, S)
    x1 = _out_proj(o_ret, o_mla, x2, p)
    return _ffn(x1, p).reshape(B, S, D_MODEL)


def kernel(x_prompt, x_sample, norm_mix_pre, w_in, ret_decay_fwd, ret_decay_bwd, mla_q_norm, w_q_up, mla_kv_norm,
           w_kv_up, w_out, norm_mix_post, norm_ffn_pre, w_ffn_up, w_ffn_down, norm_ffn_post):
    y_prompt, y_sample = x_prompt, x_sample
    tables = dict(zip(("cos_r", "sin_r", "cos_m", "sin_m"), _rope_tables(max(x_prompt.shape[1], x_sample.shape[1]))))
    for l in range(norm_mix_pre.shape[0]):
        p = _layout_params(norm_mix_pre[l], w_in[l], ret_decay_fwd[l], ret_decay_bwd[l], mla_q_norm[l], w_q_up[l],
                           mla_kv_norm[l], w_kv_up[l], w_out[l], norm_mix_post[l], norm_ffn_pre[l], w_ffn_up[l],
                           w_ffn_down[l], norm_ffn_post[l])
        p.update(tables)
        y_prompt = _layer(y_prompt, p)
        y_sample = _layer(y_sample, p)
    return (y_prompt, y_sample)
```

```python
import functools
import math

import jax
import jax.numpy as jnp
from jax import lax
from jax.experimental import pallas as pl
from jax.experimental.pallas import tpu as pltpu

D_MODEL = 2048
RET_HEADS = 8
RET_HEAD_DIM = 128
RET_WIDTH = RET_HEADS * RET_HEAD_DIM
RET_CHUNK = 128
MLA_HEADS = 8
MLA_NOPE_DIM = 128
MLA_ROPE_DIM = 64
MLA_V_DIM = 128
MLA_Q_RANK = 512
MLA_KV_RANK = 512
MLA_WIDTH = MLA_HEADS * MLA_V_DIM
FFN_DIM = 4 * D_MODEL
ROPE_BASE = 10000.0
NORM_EPS = 1e-6

LANES = 128
MLA_QK_DIM = 2 * LANES
BF16_SUBLANES = 16
V_EXT_ROWS = MLA_V_DIM + BF16_SUBLANES
IN_COLS = 4 * RET_WIDTH + MLA_Q_RANK + MLA_KV_RANK + MLA_ROPE_DIM
VMEM_LIMIT_BYTES = 60000 * 1024

F32 = jnp.float32
Q_SCALE_LOG2 = (MLA_NOPE_DIM + MLA_ROPE_DIM) ** -0.5 * math.log2(math.e)
BF16 = jnp.bfloat16

TM_IN = 256
TM_OUT = 512
TM_FFN = 1024
TF_FFN = 1024
NORM_ROWS = 16
NORM_UNROLL = 8
TQ_ATT = 1024
TK_ATT = 2048
RET_BLOCK = 256
RET_UNROLL = 8
KV_UNROLL = 2


def _const_spec(shape):
    return pl.BlockSpec(shape, lambda *_: (0,) * len(shape), pipeline_mode=pl.Buffered(1))


def _rms(x, gain):
    return x * lax.rsqrt(jnp.mean(x * x, axis=-1, keepdims=True) + NORM_EPS) * gain


def _rope_slab(x, cos, sin_signed):
    return x * cos + pltpu.roll(x, LANES // 2, 1) * sin_signed


def _in_proj_kernel(x_ref, g_ref, w_ref, wkr_ref, qg_ref, kvg_ref, wq_ref, wkv_ref, tab_ref,
                    ret_ref, rkt_ref, qk_ref, vt_ref):
    rq_ref = ret_ref.at[:, 0:RET_WIDTH]
    rv_ref = ret_ref.at[:, RET_WIDTH:2 * RET_WIDTH]
    gate_ref = ret_ref.at[:, 2 * RET_WIDTH:3 * RET_WIDTH]
    qcat_ref = qk_ref.at[:, 0:MLA_HEADS * MLA_QK_DIM]
    kcat_ref = qk_ref.at[:, MLA_HEADS * MLA_QK_DIM:2 * MLA_HEADS * MLA_QK_DIM]
    x = x_ref[...]
    h = (x * g_ref[...]).astype(BF16)
    r = lax.rsqrt(jnp.mean(x * x, axis=-1, keepdims=True) + NORM_EPS)

    def proj(lo, hi):
        return jnp.dot(h, w_ref[:, lo:hi], preferred_element_type=F32) * r

    cr, sr = tab_ref[:, 0:LANES], tab_ref[:, LANES:2 * LANES]
    cm, sm = tab_ref[:, 2 * LANES:3 * LANES], tab_ref[:, 3 * LANES:4 * LANES]

    off = 4 * RET_WIDTH
    cq = _rms(proj(off, off + MLA_Q_RANK), qg_ref[...]).astype(BF16)
    qf = jnp.dot(cq, wq_ref[...], preferred_element_type=F32) * Q_SCALE_LOG2
    lane = lax.broadcasted_iota(jnp.int32, (1, LANES), 1)
    even_lanes = (lane // (MLA_ROPE_DIM // 2)) % 2 == 0
    nope_w = MLA_HEADS * MLA_NOPE_DIM
    for j in range(MLA_HEADS // 2):
        slab = _rope_slab(qf[:, nope_w + j * LANES: nope_w + (j + 1) * LANES], cm, sm)
        for par in range(2):
            hh = 2 * j + par
            base = hh * MLA_QK_DIM
            qcat_ref[:, base: base + LANES] = qf[:, hh * LANES: (hh + 1) * LANES].astype(BF16)
            keep = even_lanes if par == 0 else jnp.logical_not(even_lanes)
            qcat_ref[:, base + LANES: base + 2 * LANES] = jnp.where(keep, slab, 0.0).astype(BF16)

    off += MLA_Q_RANK
    ckv = _rms(proj(off, off + MLA_KV_RANK), kvg_ref[...]).astype(BF16)
    kvf = jnp.dot(ckv, wkv_ref[...], preferred_element_type=F32)
    kext = _rope_slab(jnp.dot(h, wkr_ref[...], preferred_element_type=F32) * r, cm, sm).astype(BF16)
    for hh in range(MLA_HEADS):
        base = hh * MLA_QK_DIM
        kcat_ref[:, base: base + LANES] = kvf[:, hh * LANES: (hh + 1) * LANES].astype(BF16)
        kcat_ref[:, base + LANES: base + 2 * LANES] = kext
    for hh in range(MLA_HEADS):
        sl = slice(nope_w + hh * LANES, nope_w + (hh + 1) * LANES)
        vt_ref[0, hh * V_EXT_ROWS: hh * V_EXT_ROWS + MLA_V_DIM, :] = kvf[:, sl].T.astype(BF16)
        vt_ref[0, hh * V_EXT_ROWS + MLA_V_DIM: (hh + 1) * V_EXT_ROWS, :] = jnp.ones(
            (BF16_SUBLANES, x_ref.shape[0]), BF16)

    q = proj(0, RET_WIDTH)
    for hh in range(RET_HEADS):
        sl = slice(hh * LANES, (hh + 1) * LANES)
        rq_ref[:, sl] = _rope_slab(q[:, sl], cr, sr).astype(BF16)

    k = proj(RET_WIDTH, 2 * RET_WIDTH)
    k_scale = RET_HEAD_DIM ** -0.5
    for hh in range(RET_HEADS):
        sl = slice(hh * LANES, (hh + 1) * LANES)
        rkt_ref[0, sl, :] = (_rope_slab(k[:, sl], cr, sr) * k_scale).T.astype(BF16)

    rv_ref[...] = proj(2 * RET_WIDTH, 3 * RET_WIDTH).astype(BF16)

    g = proj(3 * RET_WIDTH, 4 * RET_WIDTH)
    gate_ref[...] = (g / (1.0 + jnp.exp(-g))).astype(BF16)


def _in_proj(x2, B, S, p):
    T = B * S
    tm = TM_IN
    assert S % tm == 0, (S, tm)
    ns = S // tm
    row = lambda i: (i, 0)
    pos = lambda i: (i % ns, 0)
    tpose = lambda i: (i // ns, 0, i % ns)
    bf = lambda n: jax.ShapeDtypeStruct((T, n), BF16)
    bft = jax.ShapeDtypeStruct((B, RET_WIDTH, S), BF16)
    bfv = jax.ShapeDtypeStruct((B, MLA_HEADS * V_EXT_ROWS, S), BF16)
    return pl.pallas_call(
        _in_proj_kernel,
        grid=(T // tm,),
        in_specs=[
            pl.BlockSpec((tm, D_MODEL), row),
            _const_spec((1, D_MODEL)),
            _const_spec((D_MODEL, IN_COLS)),
            _const_spec((D_MODEL, LANES)),
            _const_spec((1, MLA_Q_RANK)),
            _const_spec((1, MLA_KV_RANK)),
            _const_spec((MLA_Q_RANK, MLA_HEADS * (MLA_NOPE_DIM + MLA_ROPE_DIM))),
            _const_spec((MLA_KV_RANK, MLA_HEADS * (MLA_NOPE_DIM + MLA_V_DIM))),
            pl.BlockSpec((tm, 4 * LANES), pos),
        ],
        out_specs=[
            pl.BlockSpec((tm, 3 * RET_WIDTH), row),
            pl.BlockSpec((1, RET_WIDTH, tm), tpose),
            pl.BlockSpec((tm, 2 * MLA_HEADS * MLA_QK_DIM), row),
            pl.BlockSpec((1, MLA_HEADS * V_EXT_ROWS, tm), tpose),
        ],
        out_shape=[bf(3 * RET_WIDTH), bft, bf(2 * MLA_HEADS * MLA_QK_DIM), bfv],
        compiler_params=pltpu.CompilerParams(
            dimension_semantics=("parallel",), vmem_limit_bytes=VMEM_LIMIT_BYTES),
        name="in_proj",
    )(x2, p["g_mix_pre"], p["w_in"], p["w_kr"], p["g_q"], p["g_kv"], p["w_q_up"], p["w_kv_up"],
      p["rope_tables"])


def _log_sigmoid(x):
    return jnp.minimum(x, 0.0) - jnp.log(1.0 + jnp.exp(-jnp.abs(x)))


def _retention_kernel(q_ref, kt_ref, v_ref, g_ref, lf_ref, lb_ref, o_ref, st_ref, *, n_chunks):
    C, D = RET_BLOCK, RET_HEAD_DIM
    lf = _log_sigmoid(lf_ref[0])
    lb = _log_sigmoid(lb_ref[0])
    lf1, lb1 = lf[:, 0:1], lb[:, 0:1]

    def iota(shape, dim):
        return lax.broadcasted_iota(jnp.int32, shape, dim).astype(F32)

    diff = iota((C, C), 0) - iota((C, C), 1)
    decay = jnp.exp(jnp.where(diff >= 0, lf1 * diff, -lb1 * diff))
    pos_t = iota((D, C), 1)
    wkf_t = jnp.exp(lf1 * (C - 1.0 - pos_t))
    wkb_t = jnp.exp(lb1 * pos_t)
    pos = iota((C, D), 0)
    wqf = jnp.exp(lf * (pos + 1.0))
    wqb = jnp.exp(lb * (C - pos))
    df = jnp.exp(lf * C)
    db = jnp.exp(lb * C)

    def chunk_kv(c, w_t):
        off = pl.multiple_of(c * C, C)
        kt = kt_ref[0, :, pl.ds(off, C)].astype(F32)
        return jnp.dot((kt * w_t).astype(BF16), v_ref[0, pl.ds(off, C), :], preferred_element_type=F32)

    def scan_states(t, states):
        sf, sb = states
        cb = n_chunks - 1 - t
        st_ref[t, 0:D, :] = sf.astype(BF16)
        st_ref[cb, D:2 * D, :] = sb.astype(BF16)
        return sf * df + chunk_kv(t, wkf_t), sb * db + chunk_kv(cb, wkb_t)

    zero = jnp.zeros((D, D), F32)
    lax.fori_loop(0, n_chunks, scan_states, (zero, zero), unroll=RET_UNROLL)

    def out_chunk(c, carry):
        off = pl.multiple_of(c * C, C)
        q = q_ref[0, pl.ds(off, C), :]
        v = v_ref[0, pl.ds(off, C), :]
        s = jnp.dot(q, kt_ref[0, :, pl.ds(off, C)], preferred_element_type=F32)
        o = jnp.dot((s * decay).astype(BF16), v, preferred_element_type=F32)
        qf = q.astype(F32)
        qq = jnp.concatenate([(qf * wqf).astype(BF16), (qf * wqb).astype(BF16)], axis=1)
        o = o + jnp.dot(qq, st_ref[c], preferred_element_type=F32)
        o = o * lax.rsqrt(jnp.mean(o * o, axis=-1, keepdims=True) + NORM_EPS)
        o_ref[0, pl.ds(off, C), :] = (o * g_ref[0, pl.ds(off, C), :].astype(F32)).astype(BF16)
        return carry

    lax.fori_loop(0, n_chunks, out_chunk, 0, unroll=RET_UNROLL)


def _retention(ret, rkt, lf, lb, B, S):
    assert S % RET_BLOCK == 0, (S, RET_BLOCK)
    n = S // RET_BLOCK
    tok = lambda part: pl.BlockSpec((1, S, LANES), lambda b, h: (b, 0, part * RET_HEADS + h))
    tok_t = pl.BlockSpec((1, LANES, S), lambda b, h: (b, h, 0))
    dec = pl.BlockSpec((1, 1, LANES), lambda b, h: (h, 0, 0))
    return pl.pallas_call(
        functools.partial(_retention_kernel, n_chunks=n),
        grid=(B, RET_HEADS),
        in_specs=[tok(0), tok_t, tok(1), tok(2), dec, dec],
        out_specs=tok(0),
        out_shape=jax.ShapeDtypeStruct((B, S, RET_WIDTH), BF16),
        scratch_shapes=[pltpu.VMEM((n, 2 * RET_HEAD_DIM, RET_HEAD_DIM), BF16)],
        compiler_params=pltpu.CompilerParams(
            dimension_semantics=("parallel", "parallel"), vmem_limit_bytes=VMEM_LIMIT_BYTES),
        name="retention",
    )(ret, rkt, ret, ret, lf, lb)


def _attention_kernel(q_ref, k_ref, vt_ref, o_ref, acc_ref, s_ref, mb_ref, *, n_q, n_kv, tq, tk):
    def produce(qi, j, slot):
        qoff = pl.multiple_of(qi * tq, tq)
        koff = pl.multiple_of(j * tk, tk)
        s = lax.dot_general(k_ref[0, pl.ds(koff, tk), :], q_ref[0, pl.ds(qoff, tq), :],
                            (((1,), (1,)), ((), ())), preferred_element_type=F32)
        s_ref[slot] = s
        mb_ref[slot] = jnp.max(s, axis=0, keepdims=True)

    def consume(j, slot, m):
        koff = pl.multiple_of(j * tk, tk)
        m_new = jnp.maximum(m, mb_ref[slot])
        alpha = jnp.exp2(m - m_new)
        p = jnp.exp2(s_ref[slot] - m_new).astype(BF16)
        pv = jnp.dot(vt_ref[0, :, pl.ds(koff, tk)], p, preferred_element_type=F32)
        acc_ref[...] = alpha * acc_ref[...] + pv
        return m_new

    def query_block(qi, carry, last_q):
        def group(g, m, last):
            j0 = g * KV_UNROLL
            for i in range(KV_UNROLL):
                if not (last and i == KV_UNROLL - 1):
                    produce(qi, j0 + i + 1, (i + 1) % 2)
                elif not last_q:
                    produce(qi + 1, 0, 0)
                m = consume(j0 + i, i % 2, m)
            return m

        n_groups = n_kv // KV_UNROLL
        acc_ref[...] = jnp.zeros_like(acc_ref)
        m = jnp.full((1, tq), -jnp.inf, F32)
        if n_groups > 1:
            m = lax.fori_loop(0, n_groups - 1, functools.partial(group, last=False), m)
        group(n_groups - 1, m, last=True)
        qoff = pl.multiple_of(qi * tq, tq)
        acc = acc_ref[...]
        o = acc[0:MLA_V_DIM, :] / acc[MLA_V_DIM:MLA_V_DIM + 1, :]
        o_ref[0, pl.ds(qoff, tq), :] = o.T.astype(BF16)
        return carry

    produce(0, 0, 0)
    if n_q > 1:
        lax.fori_loop(0, n_q - 1, functools.partial(query_block, last_q=False), 0)
    query_block(n_q - 1, 0, last_q=True)


def _attention(qk, vt, B, S):
    tq = TQ_ATT
    tk = min(TK_ATT, S // KV_UNROLL)
    assert S % tq == 0 and S % tk == 0 and KV_UNROLL % 2 == 0 and (S // tk) % KV_UNROLL == 0, (S, tq, tk)
    q_spec = pl.BlockSpec((1, S, MLA_QK_DIM), lambda b, h: (b, 0, h))
    k_spec = pl.BlockSpec((1, S, MLA_QK_DIM), lambda b, h: (b, 0, MLA_HEADS + h))
    return pl.pallas_call(
        functools.partial(_attention_kernel, n_q=S // tq, n_kv=S // tk, tq=tq, tk=tk),
        grid=(B, MLA_HEADS),
        in_specs=[q_spec, k_spec, pl.BlockSpec((1, V_EXT_ROWS, S), lambda b, h: (b, h, 0))],
        out_specs=pl.BlockSpec((1, S, MLA_V_DIM), lambda b, h: (b, 0, h)),
        out_shape=jax.ShapeDtypeStruct((B, S, MLA_WIDTH), BF16),
        scratch_shapes=[pltpu.VMEM((V_EXT_ROWS, tq), F32), pltpu.VMEM((2, tk, tq), F32),
                        pltpu.VMEM((2, 1, tq), F32)],
        compiler_params=pltpu.CompilerParams(
            dimension_semantics=("parallel", "parallel"), vmem_limit_bytes=VMEM_LIMIT_BYTES),
        name="mla_attention",
    )(qk, qk, vt)


def _out_proj_kernel(ret_ref, mla_ref, x_ref, w_ref, g_ref, o_ref):
    mix = jnp.dot(ret_ref[...], w_ref[0:RET_WIDTH, :], preferred_element_type=F32)
    mix = mix + jnp.dot(mla_ref[...], w_ref[RET_WIDTH:, :], preferred_element_type=F32)
    o_ref[...] = x_ref[...] + _rms(mix, g_ref[...])


def _out_proj(o_ret, o_mla, x2, p):
    T = x2.shape[0]
    tm = TM_OUT
    assert T % tm == 0, (T, tm)
    row = lambda i: (i, 0)
    return pl.pallas_call(
        _out_proj_kernel,
        grid=(T // tm,),
        in_specs=[
            pl.BlockSpec((tm, RET_WIDTH), row),
            pl.BlockSpec((tm, MLA_WIDTH), row),
            pl.BlockSpec((tm, D_MODEL), row),
            _const_spec((RET_WIDTH + MLA_WIDTH, D_MODEL)),
            _const_spec((1, D_MODEL)),
        ],
        out_specs=pl.BlockSpec((tm, D_MODEL), row),
        out_shape=jax.ShapeDtypeStruct((T, D_MODEL), F32),
        compiler_params=pltpu.CompilerParams(
            dimension_semantics=("parallel",), vmem_limit_bytes=VMEM_LIMIT_BYTES),
        name="out_proj",
    )(o_ret.reshape(T, RET_WIDTH), o_mla.reshape(T, MLA_WIDTH), x2, p["w_out"], p["g_mix_post"])


def _ffn_kernel(x_ref, gpre_ref, wup_ref, wdn_ref, gpost_ref, o_ref, h_ref):
    j = pl.program_id(1)

    n_chunks = x_ref.shape[0] // NORM_ROWS

    @pl.when(j == 0)
    def _():
        def pre(r, carry):
            rows = pl.ds(pl.multiple_of(r * NORM_ROWS, NORM_ROWS), NORM_ROWS)
            h_ref[rows, :] = _rms(x_ref[rows, :], gpre_ref[...]).astype(BF16)
            o_ref[rows, :] = jnp.zeros((NORM_ROWS, D_MODEL), F32)
            return carry
        lax.fori_loop(0, n_chunks, pre, 0, unroll=NORM_UNROLL)

    a = jnp.maximum(jnp.dot(h_ref[...], wup_ref[...], preferred_element_type=F32), 0.0)
    o_ref[...] += jnp.dot((a * a).astype(BF16), wdn_ref[...], preferred_element_type=F32)

    @pl.when(j == pl.num_programs(1) - 1)
    def _():
        for r in range(n_chunks):
            rows = slice(r * NORM_ROWS, (r + 1) * NORM_ROWS)
            o_ref[rows, :] = x_ref[rows, :] + _rms(o_ref[rows, :], gpost_ref[...])


def _ffn(x1, p):
    T = x1.shape[0]
    tm, tf = TM_FFN, TF_FFN
    assert T % tm == 0 and FFN_DIM % tf == 0 and tm % NORM_ROWS == 0, (T, tm, tf)
    return pl.pallas_call(
        _ffn_kernel,
        grid=(T // tm, FFN_DIM // tf),
        in_specs=[
            pl.BlockSpec((tm, D_MODEL), lambda i, j: (i, 0)),
            _const_spec((1, D_MODEL)),
            pl.BlockSpec((D_MODEL, tf), lambda i, j: (0, j)),
            pl.BlockSpec((tf, D_MODEL), lambda i, j: (j, 0)),
            _const_spec((1, D_MODEL)),
        ],
        out_specs=pl.BlockSpec((tm, D_MODEL), lambda i, j: (i, 0)),
        out_shape=jax.ShapeDtypeStruct((T, D_MODEL), F32),
        scratch_shapes=[pltpu.VMEM((tm, D_MODEL), BF16)],
        compiler_params=pltpu.CompilerParams(
            dimension_semantics=("parallel", "arbitrary"), vmem_limit_bytes=VMEM_LIMIT_BYTES),
        name="ffn",
    )(x1, p["g_ffn_pre"], p["w_ffn_up"], p["w_ffn_down"], p["g_ffn_post"])


def _rope_tables(S):
    lo_n = 64

    def cos_sin(d):
        inv = (ROPE_BASE ** (-jnp.arange(0, d, 2, dtype=F32) / d))[None, :]
        a_hi = jnp.arange(0, S, lo_n, dtype=F32)[:, None] * inv
        a_lo = jnp.arange(lo_n, dtype=F32)[:, None] * inv
        ch, sh = jnp.cos(a_hi)[:, None, :], jnp.sin(a_hi)[:, None, :]
        cl, sl = jnp.cos(a_lo)[None, :, :], jnp.sin(a_lo)[None, :, :]
        return (ch * cl - sh * sl).reshape(S, d // 2), (sh * cl + ch * sl).reshape(S, d // 2)

    c_r, s_r = cos_sin(RET_HEAD_DIM)
    c_m, s_m = cos_sin(MLA_ROPE_DIM)
    cos_r = jnp.concatenate([c_r] * 2, axis=1)
    sin_r = jnp.concatenate([-s_r, s_r], axis=1)
    cos_m = jnp.concatenate([c_m] * 4, axis=1)
    sin_m = jnp.concatenate([-s_m] * 2 + [s_m] * 2, axis=1)
    return jnp.concatenate([cos_r, sin_r, cos_m, sin_m], axis=1)


def _layout_params(norm_mix_pre, w_in, ret_decay_fwd, ret_decay_bwd, mla_q_norm, w_q_up, mla_kv_norm, w_kv_up,
                   w_out, norm_mix_post, norm_ffn_pre, w_ffn_up, w_ffn_down, norm_ffn_post):
    half = MLA_ROPE_DIM // 2
    main = 4 * RET_WIDTH + MLA_Q_RANK + MLA_KV_RANK
    kr1, kr2 = w_in[:, main: main + half], w_in[:, main + half:]
    w_kr = jnp.concatenate([kr1, kr1, kr2, kr2], axis=1).astype(BF16)

    per_head = MLA_NOPE_DIM + MLA_ROPE_DIM
    wq = w_q_up.reshape(MLA_Q_RANK, MLA_HEADS, per_head)
    nope = wq[:, :, :MLA_NOPE_DIM].reshape(MLA_Q_RANK, MLA_HEADS * MLA_NOPE_DIM)
    x1 = wq[:, :, MLA_NOPE_DIM: MLA_NOPE_DIM + half].reshape(MLA_Q_RANK, MLA_HEADS // 2, 2 * half)
    x2 = wq[:, :, MLA_NOPE_DIM + half:].reshape(MLA_Q_RANK, MLA_HEADS // 2, 2 * half)
    slabs = jnp.concatenate([x1, x2], axis=2).reshape(MLA_Q_RANK, MLA_HEADS * MLA_ROPE_DIM)
    w_q_p = jnp.concatenate([nope, slabs], axis=1).astype(BF16)

    wkv = w_kv_up.reshape(MLA_KV_RANK, MLA_HEADS, MLA_NOPE_DIM + MLA_V_DIM)
    w_kv_p = jnp.concatenate([
        wkv[:, :, :MLA_NOPE_DIM].reshape(MLA_KV_RANK, MLA_HEADS * MLA_NOPE_DIM),
        wkv[:, :, MLA_NOPE_DIM:].reshape(MLA_KV_RANK, MLA_HEADS * MLA_V_DIM)], axis=1).astype(BF16)

    bcast = lambda v: jnp.broadcast_to(v.astype(F32)[:, None, None], (RET_HEADS, 1, LANES))
    return {
        "g_mix_pre": norm_mix_pre.reshape(1, D_MODEL), "w_in": w_in.astype(BF16), "w_kr": w_kr,
        "lf": bcast(ret_decay_fwd), "lb": bcast(ret_decay_bwd),
        "g_q": mla_q_norm.reshape(1, MLA_Q_RANK), "w_q_up": w_q_p,
        "g_kv": mla_kv_norm.reshape(1, MLA_KV_RANK), "w_kv_up": w_kv_p,
        "w_out": w_out.astype(BF16), "g_mix_post": norm_mix_post.reshape(1, D_MODEL),
        "g_ffn_pre": norm_ffn_pre.reshape(1, D_MODEL), "w_ffn_up": w_ffn_up.astype(BF16),
        "w_ffn_down": w_ffn_down.astype(BF16), "g_ffn_post": norm_ffn_post.reshape(1, D_MODEL),
    }


def _layer(x, p):
    B, S, _ = x.shape
    x2 = x.reshape(B * S, D_MODEL)
    ret, rkt, qk, vt = _in_proj(x2, B, S, p)
    o_ret = _retention(ret.reshape(B, S, 3 * RET_WIDTH), rkt, p["lf"], p["lb"], B, S)
    o_mla = _attention(qk.reshape(B, S, 2 * MLA_HEADS * MLA_QK_DIM), vt, B, S)
    x1 = _out_proj(o_ret, o_mla, x2, p)
    return _ffn(x1, p).reshape(B, S, D_MODEL)


def kernel(x_prompt, x_sample, norm_mix_pre, w_in, ret_decay_fwd, ret_decay_bwd, mla_q_norm, w_q_up, mla_kv_norm,
           w_kv_up, w_out, norm_mix_post, norm_ffn_pre, w_ffn_up, w_ffn_down, norm_ffn_post):
    y_prompt, y_sample = x_prompt, x_sample
    tables = _rope_tables(max(x_prompt.shape[1], x_sample.shape[1]))
    for l in range(norm_mix_pre.shape[0]):
        p = _layout_params(norm_mix_pre[l], w_in[l], ret_decay_fwd[l], ret_decay_bwd[l], mla_q_norm[l], w_q_up[l],
                           mla_kv_norm[l], w_kv_up[l], w_out[l], norm_mix_post[l], norm_ffn_pre[l], w_ffn_up[l],
                           w_ffn_down[l], norm_ffn_post[l])
        p["rope_tables"] = tables
        y_prompt = _layer(y_prompt, p)
        y_sample = _layer(y_sample, p)
    return (y_prompt, y_sample)
```

```python
import functools
import math

import jax
import jax.numpy as jnp
from jax import lax
from jax.experimental import pallas as pl
from jax.experimental.pallas import tpu as pltpu

D_MODEL = 2048
RET_HEADS = 8
RET_HEAD_DIM = 128
RET_WIDTH = RET_HEADS * RET_HEAD_DIM
RET_CHUNK = 128
MLA_HEADS = 8
MLA_NOPE_DIM = 128
MLA_ROPE_DIM = 64
MLA_V_DIM = 128
MLA_Q_RANK = 512
MLA_KV_RANK = 512
MLA_WIDTH = MLA_HEADS * MLA_V_DIM
FFN_DIM = 4 * D_MODEL
ROPE_BASE = 10000.0
NORM_EPS = 1e-6

LANES = 128
MLA_QK_DIM = 2 * LANES
BF16_SUBLANES = 16
V_EXT_ROWS = MLA_V_DIM + BF16_SUBLANES
IN_COLS = 4 * RET_WIDTH + MLA_Q_RANK + MLA_KV_RANK + MLA_ROPE_DIM
VMEM_LIMIT_BYTES = 60000 * 1024

F32 = jnp.float32
Q_SCALE_LOG2 = (MLA_NOPE_DIM + MLA_ROPE_DIM) ** -0.5 * math.log2(math.e)
BF16 = jnp.bfloat16

TM_IN = 256
TM_OUT = 512
TM_FFN = 1024
TF_FFN = 1024
NORM_ROWS = 16
NORM_UNROLL = 8
SCORE_TILE = 2048 * 1024
TK_ATT = 4096
RET_BLOCK = 256
RET_UNROLL = 8
KV_UNROLL = 2


def _const_spec(shape):
    return pl.BlockSpec(shape, lambda *_: (0,) * len(shape), pipeline_mode=pl.Buffered(1))


def _rms(x, gain):
    return x * lax.rsqrt(jnp.mean(x * x, axis=-1, keepdims=True) + NORM_EPS) * gain


def _rope_slab(x, cos, sin_signed):
    return x * cos + pltpu.roll(x, LANES // 2, 1) * sin_signed


def _in_proj_kernel(x_ref, g_ref, w_ref, wkr_ref, qg_ref, kvg_ref, wq_ref, wkv_ref, cr_ref, sr_ref, cm_ref, sm_ref,
                    rq_ref, rkt_ref, rv_ref, gate_ref, qcat_ref, kcat_ref, vt_ref):
    x = x_ref[...]
    h = (x * g_ref[...]).astype(BF16)
    r = lax.rsqrt(jnp.mean(x * x, axis=-1, keepdims=True) + NORM_EPS)

    def proj(lo, hi):
        return jnp.dot(h, w_ref[:, lo:hi], preferred_element_type=F32) * r

    cr, sr = cr_ref[...], sr_ref[...]
    cm, sm = cm_ref[...], sm_ref[...]

    off = 4 * RET_WIDTH
    cq = _rms(proj(off, off + MLA_Q_RANK), qg_ref[...]).astype(BF16)
    qf = jnp.dot(cq, wq_ref[...], preferred_element_type=F32) * Q_SCALE_LOG2
    lane = lax.broadcasted_iota(jnp.int32, (1, LANES), 1)
    even_lanes = (lane // (MLA_ROPE_DIM // 2)) % 2 == 0
    nope_w = MLA_HEADS * MLA_NOPE_DIM
    for j in range(MLA_HEADS // 2):
        slab = _rope_slab(qf[:, nope_w + j * LANES: nope_w + (j + 1) * LANES], cm, sm)
        for par in range(2):
            hh = 2 * j + par
            base = hh * MLA_QK_DIM
            qcat_ref[:, base: base + LANES] = qf[:, hh * LANES: (hh + 1) * LANES].astype(BF16)
            keep = even_lanes if par == 0 else jnp.logical_not(even_lanes)
            qcat_ref[:, base + LANES: base + 2 * LANES] = jnp.where(keep, slab, 0.0).astype(BF16)

    off += MLA_Q_RANK
    ckv = _rms(proj(off, off + MLA_KV_RANK), kvg_ref[...]).astype(BF16)
    kvf = jnp.dot(ckv, wkv_ref[...], preferred_element_type=F32)
    kext = _rope_slab(jnp.dot(h, wkr_ref[...], preferred_element_type=F32) * r, cm, sm).astype(BF16)
    for hh in range(MLA_HEADS):
        base = hh * MLA_QK_DIM
        kcat_ref[:, base: base + LANES] = kvf[:, hh * LANES: (hh + 1) * LANES].astype(BF16)
        kcat_ref[:, base + LANES: base + 2 * LANES] = kext
    for hh in range(MLA_HEADS):
        sl = slice(nope_w + hh * LANES, nope_w + (hh + 1) * LANES)
        vt_ref[0, hh * V_EXT_ROWS: hh * V_EXT_ROWS + MLA_V_DIM, :] = kvf[:, sl].T.astype(BF16)
        vt_ref[0, hh * V_EXT_ROWS + MLA_V_DIM: (hh + 1) * V_EXT_ROWS, :] = jnp.ones(
            (BF16_SUBLANES, x_ref.shape[0]), BF16)

    q = proj(0, RET_WIDTH)
    for hh in range(RET_HEADS):
        sl = slice(hh * LANES, (hh + 1) * LANES)
        rq_ref[:, sl] = _rope_slab(q[:, sl], cr, sr).astype(BF16)

    k = proj(RET_WIDTH, 2 * RET_WIDTH)
    k_scale = RET_HEAD_DIM ** -0.5
    for hh in range(RET_HEADS):
        sl = slice(hh * LANES, (hh + 1) * LANES)
        rkt_ref[0, sl, :] = (_rope_slab(k[:, sl], cr, sr) * k_scale).T.astype(BF16)

    rv_ref[...] = proj(2 * RET_WIDTH, 3 * RET_WIDTH).astype(BF16)

    g = proj(3 * RET_WIDTH, 4 * RET_WIDTH)
    gate_ref[...] = (g / (1.0 + jnp.exp(-g))).astype(BF16)


def _in_proj(x2, B, S, p):
    T = B * S
    tm = TM_IN
    assert S % tm == 0, (S, tm)
    ns = S // tm
    row = lambda i: (i, 0)
    pos = lambda i: (i % ns, 0)
    tpose = lambda i: (i // ns, 0, i % ns)
    bf = lambda n: jax.ShapeDtypeStruct((T, n), BF16)
    bft = jax.ShapeDtypeStruct((B, RET_WIDTH, S), BF16)
    bfv = jax.ShapeDtypeStruct((B, MLA_HEADS * V_EXT_ROWS, S), BF16)
    return pl.pallas_call(
        _in_proj_kernel,
        grid=(T // tm,),
        in_specs=[
            pl.BlockSpec((tm, D_MODEL), row),
            _const_spec((1, D_MODEL)),
            _const_spec((D_MODEL, IN_COLS)),
            _const_spec((D_MODEL, LANES)),
            _const_spec((1, MLA_Q_RANK)),
            _const_spec((1, MLA_KV_RANK)),
            _const_spec((MLA_Q_RANK, MLA_HEADS * (MLA_NOPE_DIM + MLA_ROPE_DIM))),
            _const_spec((MLA_KV_RANK, MLA_HEADS * (MLA_NOPE_DIM + MLA_V_DIM))),
            pl.BlockSpec((tm, LANES), pos),
            pl.BlockSpec((tm, LANES), pos),
            pl.BlockSpec((tm, LANES), pos),
            pl.BlockSpec((tm, LANES), pos),
        ],
        out_specs=[
            pl.BlockSpec((tm, RET_WIDTH), row),
            pl.BlockSpec((1, RET_WIDTH, tm), tpose),
            pl.BlockSpec((tm, RET_WIDTH), row),
            pl.BlockSpec((tm, RET_WIDTH), row),
            pl.BlockSpec((tm, MLA_HEADS * MLA_QK_DIM), row),
            pl.BlockSpec((tm, MLA_HEADS * MLA_QK_DIM), row),
            pl.BlockSpec((1, MLA_HEADS * V_EXT_ROWS, tm), tpose),
        ],
        out_shape=[bf(RET_WIDTH), bft, bf(RET_WIDTH), bf(RET_WIDTH),
                   bf(MLA_HEADS * MLA_QK_DIM), bf(MLA_HEADS * MLA_QK_DIM), bfv],
        compiler_params=pltpu.CompilerParams(
            dimension_semantics=("parallel",), vmem_limit_bytes=VMEM_LIMIT_BYTES),
        name="in_proj",
    )(x2, p["g_mix_pre"], p["w_in"], p["w_kr"], p["g_q"], p["g_kv"], p["w_q_up"], p["w_kv_up"],
      p["cos_r"], p["sin_r"], p["cos_m"], p["sin_m"])


def _log_sigmoid(x):
    return jnp.minimum(x, 0.0) - jnp.log(1.0 + jnp.exp(-jnp.abs(x)))


def _retention_kernel(q_ref, kt_ref, v_ref, g_ref, lf_ref, lb_ref, o_ref, st_ref, *, n_chunks):
    C, D = RET_BLOCK, RET_HEAD_DIM
    lf = _log_sigmoid(lf_ref[0])
    lb = _log_sigmoid(lb_ref[0])
    lf1, lb1 = lf[:, 0:1], lb[:, 0:1]

    def iota(shape, dim):
        return lax.broadcasted_iota(jnp.int32, shape, dim).astype(F32)

    diff = iota((C, C), 0) - iota((C, C), 1)
    decay = jnp.exp(jnp.where(diff >= 0, lf1 * diff, -lb1 * diff))
    pos_t = iota((D, C), 1)
    wkf_t = jnp.exp(lf1 * (C - 1.0 - pos_t))
    wkb_t = jnp.exp(lb1 * pos_t)
    pos = iota((C, D), 0)
    wqf = jnp.exp(lf * (pos + 1.0))
    wqb = jnp.exp(lb * (C - pos))
    df = jnp.exp(lf * C)
    db = jnp.exp(lb * C)

    def chunk_kv(c, w_t):
        off = pl.multiple_of(c * C, C)
        kt = kt_ref[0, :, pl.ds(off, C)].astype(F32)
        return jnp.dot((kt * w_t).astype(BF16), v_ref[0, pl.ds(off, C), :], preferred_element_type=F32)

    def scan_states(t, states):
        sf, sb = states
        cb = n_chunks - 1 - t
        st_ref[t, 0:D, :] = sf.astype(BF16)
        st_ref[cb, D:2 * D, :] = sb.astype(BF16)
        return sf * df + chunk_kv(t, wkf_t), sb * db + chunk_kv(cb, wkb_t)

    zero = jnp.zeros((D, D), F32)
    lax.fori_loop(0, n_chunks, scan_states, (zero, zero), unroll=RET_UNROLL)

    def out_chunk(c, carry):
        off = pl.multiple_of(c * C, C)
        q = q_ref[0, pl.ds(off, C), :]
        v = v_ref[0, pl.ds(off, C), :]
        s = jnp.dot(q, kt_ref[0, :, pl.ds(off, C)], preferred_element_type=F32)
        o = jnp.dot((s * decay).astype(BF16), v, preferred_element_type=F32)
        qf = q.astype(F32)
        qq = jnp.concatenate([(qf * wqf).astype(BF16), (qf * wqb).astype(BF16)], axis=1)
        o = o + jnp.dot(qq, st_ref[c], preferred_element_type=F32)
        o = o * lax.rsqrt(jnp.mean(o * o, axis=-1, keepdims=True) + NORM_EPS)
        o_ref[0, pl.ds(off, C), :] = (o * g_ref[0, pl.ds(off, C), :].astype(F32)).astype(BF16)
        return carry

    lax.fori_loop(0, n_chunks, out_chunk, 0, unroll=RET_UNROLL)


def _retention(rq, rkt, rv, gate, lf, lb, B, S):
    assert S % RET_BLOCK == 0, (S, RET_BLOCK)
    n = S // RET_BLOCK
    tok = pl.BlockSpec((1, S, LANES), lambda b, h: (b, 0, h))
    tok_t = pl.BlockSpec((1, LANES, S), lambda b, h: (b, h, 0))
    dec = pl.BlockSpec((1, 1, LANES), lambda b, h: (h, 0, 0))
    return pl.pallas_call(
        functools.partial(_retention_kernel, n_chunks=n),
        grid=(B, RET_HEADS),
        in_specs=[tok, tok_t, tok, tok, dec, dec],
        out_specs=tok,
        out_shape=jax.ShapeDtypeStruct((B, S, RET_WIDTH), BF16),
        scratch_shapes=[pltpu.VMEM((n, 2 * RET_HEAD_DIM, RET_HEAD_DIM), BF16)],
        compiler_params=pltpu.CompilerParams(
            dimension_semantics=("parallel", "parallel"), vmem_limit_bytes=VMEM_LIMIT_BYTES),
        name="retention",
    )(rq.reshape(B, S, RET_WIDTH), rkt, rv.reshape(B, S, RET_WIDTH), gate.reshape(B, S, RET_WIDTH), lf, lb)


def _attention_kernel(q_ref, k_ref, vt_ref, o_ref, acc_ref, s_ref, mb_ref, *, n_q, n_kv, tq, tk):
    def produce(qi, j, slot):
        qoff = pl.multiple_of(qi * tq, tq)
        koff = pl.multiple_of(j * tk, tk)
        s = lax.dot_general(k_ref[0, pl.ds(koff, tk), :], q_ref[0, pl.ds(qoff, tq), :],
                            (((1,), (1,)), ((), ())), preferred_element_type=F32)
        s_ref[slot] = s
        mb_ref[slot] = jnp.max(s, axis=0, keepdims=True)

    def consume(j, slot, m):
        koff = pl.multiple_of(j * tk, tk)
        m_new = jnp.maximum(m, mb_ref[slot])
        alpha = jnp.exp2(m - m_new)
        p = jnp.exp2(s_ref[slot] - m_new).astype(BF16)
        pv = jnp.dot(vt_ref[0, :, pl.ds(koff, tk)], p, preferred_element_type=F32)
        acc_ref[...] = alpha * acc_ref[...] + pv
        return m_new

    def query_block(qi, carry, last_q):
        def group(g, m, last):
            j0 = g * KV_UNROLL
            for i in range(KV_UNROLL):
                if not (last and i == KV_UNROLL - 1):
                    produce(qi, j0 + i + 1, (i + 1) % 2)
                elif not last_q:
                    produce(qi + 1, 0, 0)
                m = consume(j0 + i, i % 2, m)
            return m

        n_groups = n_kv // KV_UNROLL
        acc_ref[...] = jnp.zeros_like(acc_ref)
        m = jnp.full((1, tq), -jnp.inf, F32)
        if n_groups > 1:
            m = lax.fori_loop(0, n_groups - 1, functools.partial(group, last=False), m)
        group(n_groups - 1, m, last=True)
        qoff = pl.multiple_of(qi * tq, tq)
        acc = acc_ref[...]
        o = acc[0:MLA_V_DIM, :] / acc[MLA_V_DIM:MLA_V_DIM + 1, :]
        o_ref[0, pl.ds(qoff, tq), :] = o.T.astype(BF16)
        return carry

    produce(0, 0, 0)
    if n_q > 1:
        lax.fori_loop(0, n_q - 1, functools.partial(query_block, last_q=False), 0)
    query_block(n_q - 1, 0, last_q=True)


def _attention(qcat, kcat, vt, B, S):
    tk = min(TK_ATT, S // KV_UNROLL)
    tq = SCORE_TILE // tk
    assert S % tq == 0 and S % tk == 0 and KV_UNROLL % 2 == 0 and (S // tk) % KV_UNROLL == 0, (S, tq, tk)
    qk_spec = pl.BlockSpec((1, S, MLA_QK_DIM), lambda b, h: (b, 0, h))
    return pl.pallas_call(
        functools.partial(_attention_kernel, n_q=S // tq, n_kv=S // tk, tq=tq, tk=tk),
        grid=(B, MLA_HEADS),
        in_specs=[qk_spec, qk_spec, pl.BlockSpec((1, V_EXT_ROWS, S), lambda b, h: (b, h, 0))],
        out_specs=pl.BlockSpec((1, S, MLA_V_DIM), lambda b, h: (b, 0, h)),
        out_shape=jax.ShapeDtypeStruct((B, S, MLA_WIDTH), BF16),
        scratch_shapes=[pltpu.VMEM((V_EXT_ROWS, tq), F32), pltpu.VMEM((2, tk, tq), F32),
                        pltpu.VMEM((2, 1, tq), F32)],
        compiler_params=pltpu.CompilerParams(
            dimension_semantics=("parallel", "parallel"), vmem_limit_bytes=VMEM_LIMIT_BYTES),
        name="mla_attention",
    )(qcat.reshape(B, S, MLA_HEADS * MLA_QK_DIM), kcat.reshape(B, S, MLA_HEADS * MLA_QK_DIM), vt)


def _out_proj_kernel(ret_ref, mla_ref, x_ref, w_ref, g_ref, o_ref):
    mix = jnp.dot(ret_ref[...], w_ref[0:RET_WIDTH, :], preferred_element_type=F32)
    mix = mix + jnp.dot(mla_ref[...], w_ref[RET_WIDTH:, :], preferred_element_type=F32)
    o_ref[...] = x_ref[...] + _rms(mix, g_ref[...])


def _out_proj(o_ret, o_mla, x2, p):
    T = x2.shape[0]
    tm = TM_OUT
    assert T % tm == 0, (T, tm)
    row = lambda i: (i, 0)
    return pl.pallas_call(
        _out_proj_kernel,
        grid=(T // tm,),
        in_specs=[
            pl.BlockSpec((tm, RET_WIDTH), row),
            pl.BlockSpec((tm, MLA_WIDTH), row),
            pl.BlockSpec((tm, D_MODEL), row),
            _const_spec((RET_WIDTH + MLA_WIDTH, D_MODEL)),
            _const_spec((1, D_MODEL)),
        ],
        out_specs=pl.BlockSpec((tm, D_MODEL), row),
        out_shape=jax.ShapeDtypeStruct((T, D_MODEL), F32),
        compiler_params=pltpu.CompilerParams(
            dimension_semantics=("parallel",), vmem_limit_bytes=VMEM_LIMIT_BYTES),
        name="out_proj",
    )(o_ret.reshape(T, RET_WIDTH), o_mla.reshape(T, MLA_WIDTH), x2, p["w_out"], p["g_mix_post"])


def _ffn_kernel(x_ref, gpre_ref, wup_ref, wdn_ref, gpost_ref, o_ref, h_ref):
    j = pl.program_id(1)

    n_chunks = x_ref.shape[0] // NORM_ROWS

    @pl.when(j == 0)
    def _():
        def pre(r, carry):
            rows = pl.ds(pl.multiple_of(r * NORM_ROWS, NORM_ROWS), NORM_ROWS)
            h_ref[rows, :] = _rms(x_ref[rows, :], gpre_ref[...]).astype(BF16)
            o_ref[rows, :] = jnp.zeros((NORM_ROWS, D_MODEL), F32)
            return carry
        lax.fori_loop(0, n_chunks, pre, 0, unroll=NORM_UNROLL)

    a = jnp.maximum(jnp.dot(h_ref[...], wup_ref[...], preferred_element_type=F32), 0.0)
    o_ref[...] += jnp.dot((a * a).astype(BF16), wdn_ref[...], preferred_element_type=F32)

    @pl.when(j == pl.num_programs(1) - 1)
    def _():
        for r in range(n_chunks):
            rows = slice(r * NORM_ROWS, (r + 1) * NORM_ROWS)
            o_ref[rows, :] = x_ref[rows, :] + _rms(o_ref[rows, :], gpost_ref[...])


def _ffn(x1, p):
    T = x1.shape[0]
    tm, tf = TM_FFN, TF_FFN
    assert T % tm == 0 and FFN_DIM % tf == 0 and tm % NORM_ROWS == 0, (T, tm, tf)
    return pl.pallas_call(
        _ffn_kernel,
        grid=(T // tm, FFN_DIM // tf),
        in_specs=[
            pl.BlockSpec((tm, D_MODEL), lambda i, j: (i, 0)),
            _const_spec((1, D_MODEL)),
            pl.BlockSpec((D_MODEL, tf), lambda i, j: (0, j)),
            pl.BlockSpec((tf, D_MODEL), lambda i, j: (j, 0)),
            _const_spec((1, D_MODEL)),
        ],
        out_specs=pl.BlockSpec((tm, D_MODEL), lambda i, j: (i, 0)),
        out_shape=jax.ShapeDtypeStruct((T, D_MODEL), F32),
        scratch_shapes=[pltpu.VMEM((tm, D_MODEL), BF16)],
        compiler_params=pltpu.CompilerParams(
            dimension_semantics=("parallel", "arbitrary"), vmem_limit_bytes=VMEM_LIMIT_BYTES),
        name="ffn",
    )(x1, p["g_ffn_pre"], p["w_ffn_up"], p["w_ffn_down"], p["g_ffn_post"])


def _rope_tables(S):
    lo_n = 64

    def cos_sin(d):
        inv = (ROPE_BASE ** (-jnp.arange(0, d, 2, dtype=F32) / d))[None, :]
        a_hi = jnp.arange(0, S, lo_n, dtype=F32)[:, None] * inv
        a_lo = jnp.arange(lo_n, dtype=F32)[:, None] * inv
        ch, sh = jnp.cos(a_hi)[:, None, :], jnp.sin(a_hi)[:, None, :]
        cl, sl = jnp.cos(a_lo)[None, :, :], jnp.sin(a_lo)[None, :, :]
        return (ch * cl - sh * sl).reshape(S, d // 2), (sh * cl + ch * sl).reshape(S, d // 2)

    c_r, s_r = cos_sin(RET_HEAD_DIM)
    c_m, s_m = cos_sin(MLA_ROPE_DIM)
    cos_r = jnp.concatenate([c_r] * 2, axis=1)
    sin_r = jnp.concatenate([-s_r, s_r], axis=1)
    cos_m = jnp.concatenate([c_m] * 4, axis=1)
    sin_m = jnp.concatenate([-s_m] * 2 + [s_m] * 2, axis=1)
    return cos_r, sin_r, cos_m, sin_m


def _layout_params(norm_mix_pre, w_in, ret_decay_fwd, ret_decay_bwd, mla_q_norm, w_q_up, mla_kv_norm, w_kv_up,
                   w_out, norm_mix_post, norm_ffn_pre, w_ffn_up, w_ffn_down, norm_ffn_post):
    half = MLA_ROPE_DIM // 2
    main = 4 * RET_WIDTH + MLA_Q_RANK + MLA_KV_RANK
    kr1, kr2 = w_in[:, main: main + half], w_in[:, main + half:]
    w_kr = jnp.concatenate([kr1, kr1, kr2, kr2], axis=1).astype(BF16)

    per_head = MLA_NOPE_DIM + MLA_ROPE_DIM
    wq = w_q_up.reshape(MLA_Q_RANK, MLA_HEADS, per_head)
    nope = wq[:, :, :MLA_NOPE_DIM].reshape(MLA_Q_RANK, MLA_HEADS * MLA_NOPE_DIM)
    x1 = wq[:, :, MLA_NOPE_DIM: MLA_NOPE_DIM + half].reshape(MLA_Q_RANK, MLA_HEADS // 2, 2 * half)
    x2 = wq[:, :, MLA_NOPE_DIM + half:].reshape(MLA_Q_RANK, MLA_HEADS // 2, 2 * half)
    slabs = jnp.concatenate([x1, x2], axis=2).reshape(MLA_Q_RANK, MLA_HEADS * MLA_ROPE_DIM)
    w_q_p = jnp.concatenate([nope, slabs], axis=1).astype(BF16)

    wkv = w_kv_up.reshape(MLA_KV_RANK, MLA_HEADS, MLA_NOPE_DIM + MLA_V_DIM)
    w_kv_p = jnp.concatenate([
        wkv[:, :, :MLA_NOPE_DIM].reshape(MLA_KV_RANK, MLA_HEADS * MLA_NOPE_DIM),
        wkv[:, :, MLA_NOPE_DIM:].reshape(MLA_KV_RANK, MLA_HEADS * MLA_V_DIM)], axis=1).astype(BF16)

    bcast = lambda v: jnp.broadcast_to(v.astype(F32)[:, None, None], (RET_HEADS, 1, LANES))
    return {
        "g_mix_pre": norm_mix_pre.reshape(1, D_MODEL), "w_in": w_in.astype(BF16), "w_kr": w_kr,
        "lf": bcast(ret_decay_fwd), "lb": bcast(ret_decay_bwd),
        "g_q": mla_q_norm.reshape(1, MLA_Q_RANK), "w_q_up": w_q_p,
        "g_kv": mla_kv_norm.reshape(1, MLA_KV_RANK), "w_kv_up": w_kv_p,
        "w_out": w_out.astype(BF16), "g_mix_post": norm_mix_post.reshape(1, D_MODEL),
        "g_ffn_pre": norm_ffn_pre.reshape(1, D_MODEL), "w_ffn_up": w_ffn_up.astype(BF16),
        "w_ffn_down": w_ffn_down.astype(BF16), "g_ffn_post": norm_ffn_post.reshape(1, D_MODEL),
    }


def _layer(x, p):
    B, S, _ = x.shape
    x2 = x.reshape(B * S, D_MODEL)
    rq, rkt, rv, gate, qcat, kcat, vt = _in_proj(x2, B, S, p)
    o_ret = _retention(rq, rkt, rv, gate, p["lf"], p["lb"], B, S)
    o_mla = _attention(qcat, kcat, vt, B, S)
    x1 = _out_proj(o_ret, o_mla, x2, p)
    return _ffn(x1, p).reshape(B, S, D_MODEL)


def kernel(x_prompt, x_sample, norm_mix_pre, w_in, ret_decay_fwd, ret_decay_bwd, mla_q_norm, w_q_up, mla_kv_norm,
           w_kv_up, w_out, norm_mix_post, norm_ffn_pre, w_ffn_up, w_ffn_down, norm_ffn_post):
    y_prompt, y_sample = x_prompt, x_sample
    tables = dict(zip(("cos_r", "sin_r", "cos_m", "sin_m"), _rope_tables(max(x_prompt.shape[1], x_sample.shape[1]))))
    for l in range(norm_mix_pre.shape[0]):
        p = _layout_params(norm_mix_pre[l], w_in[l], ret_decay_fwd[l], ret_decay_bwd[l], mla_q_norm[l], w_q_up[l],
                           mla_kv_norm[l], w_kv_up[l], w_out[l], norm_mix_post[l], norm_ffn_pre[l], w_ffn_up[l],
                           w_ffn_down[l], norm_ffn_post[l])
        p.update(tables)
        y_prompt = _layer(y_prompt, p)
        y_sample = _layer(y_sample, p)
    return (y_prompt, y_sample)
```

```python
import functools
import math

import jax
import jax.numpy as jnp
from jax import lax
from jax.experimental import pallas as pl
from jax.experimental.pallas import tpu as pltpu

D_MODEL = 2048
RET_HEADS = 8
RET_HEAD_DIM = 128
RET_WIDTH = RET_HEADS * RET_HEAD_DIM
RET_CHUNK = 128
MLA_HEADS = 8
MLA_NOPE_DIM = 128
MLA_ROPE_DIM = 64
MLA_V_DIM = 128
MLA_Q_RANK = 512
MLA_KV_RANK = 512
MLA_WIDTH = MLA_HEADS * MLA_V_DIM
FFN_DIM = 4 * D_MODEL
ROPE_BASE = 10000.0
NORM_EPS = 1e-6

LANES = 128
MLA_QK_DIM = 2 * LANES
BF16_SUBLANES = 16
V_EXT_ROWS = MLA_V_DIM + BF16_SUBLANES
IN_COLS = 4 * RET_WIDTH + MLA_Q_RANK + MLA_KV_RANK + MLA_ROPE_DIM
VMEM_LIMIT_BYTES = 60000 * 1024

F32 = jnp.float32
Q_SCALE_LOG2 = (MLA_NOPE_DIM + MLA_ROPE_DIM) ** -0.5 * math.log2(math.e)
BF16 = jnp.bfloat16

TM_IN = 256
TM_OUT = 1024
OUT_SUB_ROWS = 512
TM_FFN = 1024
TF_FFN = 1024
NORM_ROWS = 16
NORM_UNROLL = 8
SCORE_TILE = 2048 * 1024
TK_ATT = 4096
RET_BLOCK = 256
RET_UNROLL = 8
KV_UNROLL = 2


def _const_spec(shape):
    return pl.BlockSpec(shape, lambda *_: (0,) * len(shape), pipeline_mode=pl.Buffered(1))


def _rms(x, gain):
    return x * lax.rsqrt(jnp.mean(x * x, axis=-1, keepdims=True) + NORM_EPS) * gain


def _rope_slab(x, cos, sin_signed):
    return x * cos + pltpu.roll(x, LANES // 2, 1) * sin_signed


def _in_proj_kernel(x_ref, g_ref, w_ref, wkr_ref, qg_ref, kvg_ref, wq_ref, wkv_ref, cr_ref, sr_ref, cm_ref, sm_ref,
                    rq_ref, rkt_ref, rv_ref, gate_ref, qcat_ref, kcat_ref, vt_ref):
    x = x_ref[...]
    h = (x * g_ref[...]).astype(BF16)
    r = lax.rsqrt(jnp.mean(x * x, axis=-1, keepdims=True) + NORM_EPS)

    def proj(lo, hi):
        return jnp.dot(h, w_ref[:, lo:hi], preferred_element_type=F32) * r

    cr, sr = cr_ref[...], sr_ref[...]
    cm, sm = cm_ref[...], sm_ref[...]

    off = 4 * RET_WIDTH
    cq = _rms(proj(off, off + MLA_Q_RANK), qg_ref[...]).astype(BF16)
    qf = jnp.dot(cq, wq_ref[...], preferred_element_type=F32) * Q_SCALE_LOG2
    lane = lax.broadcasted_iota(jnp.int32, (1, LANES), 1)
    even_lanes = (lane // (MLA_ROPE_DIM // 2)) % 2 == 0
    nope_w = MLA_HEADS * MLA_NOPE_DIM
    for j in range(MLA_HEADS // 2):
        slab = _rope_slab(qf[:, nope_w + j * LANES: nope_w + (j + 1) * LANES], cm, sm)
        for par in range(2):
            hh = 2 * j + par
            base = hh * MLA_QK_DIM
            qcat_ref[:, base: base + LANES] = qf[:, hh * LANES: (hh + 1) * LANES].astype(BF16)
            keep = even_lanes if par == 0 else jnp.logical_not(even_lanes)
            qcat_ref[:, base + LANES: base + 2 * LANES] = jnp.where(keep, slab, 0.0).astype(BF16)

    off += MLA_Q_RANK
    ckv = _rms(proj(off, off + MLA_KV_RANK), kvg_ref[...]).astype(BF16)
    kvf = jnp.dot(ckv, wkv_ref[...], preferred_element_type=F32)
    kext = _rope_slab(jnp.dot(h, wkr_ref[...], preferred_element_type=F32) * r, cm, sm).astype(BF16)
    for hh in range(MLA_HEADS):
        base = hh * MLA_QK_DIM
        kcat_ref[:, base: base + LANES] = kvf[:, hh * LANES: (hh + 1) * LANES].astype(BF16)
        kcat_ref[:, base + LANES: base + 2 * LANES] = kext
    for hh in range(MLA_HEADS):
        sl = slice(nope_w + hh * LANES, nope_w + (hh + 1) * LANES)
        vt_ref[0, hh * V_EXT_ROWS: hh * V_EXT_ROWS + MLA_V_DIM, :] = kvf[:, sl].T.astype(BF16)
        vt_ref[0, hh * V_EXT_ROWS + MLA_V_DIM: (hh + 1) * V_EXT_ROWS, :] = jnp.ones(
            (BF16_SUBLANES, x_ref.shape[0]), BF16)

    q = proj(0, RET_WIDTH)
    for hh in range(RET_HEADS):
        sl = slice(hh * LANES, (hh + 1) * LANES)
        rq_ref[:, sl] = _rope_slab(q[:, sl], cr, sr).astype(BF16)

    k = proj(RET_WIDTH, 2 * RET_WIDTH)
    k_scale = RET_HEAD_DIM ** -0.5
    for hh in range(RET_HEADS):
        sl = slice(hh * LANES, (hh + 1) * LANES)
        rkt_ref[0, sl, :] = (_rope_slab(k[:, sl], cr, sr) * k_scale).T.astype(BF16)

    rv_ref[...] = proj(2 * RET_WIDTH, 3 * RET_WIDTH).astype(BF16)

    g = proj(3 * RET_WIDTH, 4 * RET_WIDTH)
    gate_ref[...] = (g / (1.0 + jnp.exp(-g))).astype(BF16)


def _in_proj(x2, B, S, p):
    T = B * S
    tm = TM_IN
    assert S % tm == 0, (S, tm)
    ns = S // tm
    row = lambda i: (i, 0)
    pos = lambda i: (i % ns, 0)
    tpose = lambda i: (i // ns, 0, i % ns)
    bf = lambda n: jax.ShapeDtypeStruct((T, n), BF16)
    bft = jax.ShapeDtypeStruct((B, RET_WIDTH, S), BF16)
    bfv = jax.ShapeDtypeStruct((B, MLA_HEADS * V_EXT_ROWS, S), BF16)
    return pl.pallas_call(
        _in_proj_kernel,
        grid=(T // tm,),
        in_specs=[
            pl.BlockSpec((tm, D_MODEL), row),
            _const_spec((1, D_MODEL)),
            _const_spec((D_MODEL, IN_COLS)),
            _const_spec((D_MODEL, LANES)),
            _const_spec((1, MLA_Q_RANK)),
            _const_spec((1, MLA_KV_RANK)),
            _const_spec((MLA_Q_RANK, MLA_HEADS * (MLA_NOPE_DIM + MLA_ROPE_DIM))),
            _const_spec((MLA_KV_RANK, MLA_HEADS * (MLA_NOPE_DIM + MLA_V_DIM))),
            pl.BlockSpec((tm, LANES), pos),
            pl.BlockSpec((tm, LANES), pos),
            pl.BlockSpec((tm, LANES), pos),
            pl.BlockSpec((tm, LANES), pos),
        ],
        out_specs=[
            pl.BlockSpec((tm, RET_WIDTH), row),
            pl.BlockSpec((1, RET_WIDTH, tm), tpose),
            pl.BlockSpec((tm, RET_WIDTH), row),
            pl.BlockSpec((tm, RET_WIDTH), row),
            pl.BlockSpec((tm, MLA_HEADS * MLA_QK_DIM), row),
            pl.BlockSpec((tm, MLA_HEADS * MLA_QK_DIM), row),
            pl.BlockSpec((1, MLA_HEADS * V_EXT_ROWS, tm), tpose),
        ],
        out_shape=[bf(RET_WIDTH), bft, bf(RET_WIDTH), bf(RET_WIDTH),
                   bf(MLA_HEADS * MLA_QK_DIM), bf(MLA_HEADS * MLA_QK_DIM), bfv],
        compiler_params=pltpu.CompilerParams(
            dimension_semantics=("parallel",), vmem_limit_bytes=VMEM_LIMIT_BYTES),
        name="in_proj",
    )(x2, p["g_mix_pre"], p["w_in"], p["w_kr"], p["g_q"], p["g_kv"], p["w_q_up"], p["w_kv_up"],
      p["cos_r"], p["sin_r"], p["cos_m"], p["sin_m"])


def _log_sigmoid(x):
    return jnp.minimum(x, 0.0) - jnp.log(1.0 + jnp.exp(-jnp.abs(x)))


def _retention_kernel(q_ref, kt_ref, v_ref, g_ref, lf_ref, lb_ref, o_ref, st_ref, *, n_chunks):
    C, D = RET_BLOCK, RET_HEAD_DIM
    lf = _log_sigmoid(lf_ref[0])
    lb = _log_sigmoid(lb_ref[0])
    lf1, lb1 = lf[:, 0:1], lb[:, 0:1]

    def iota(shape, dim):
        return lax.broadcasted_iota(jnp.int32, shape, dim).astype(F32)

    diff = iota((C, C), 0) - iota((C, C), 1)
    decay = jnp.exp(jnp.where(diff >= 0, lf1 * diff, -lb1 * diff))
    pos_t = iota((D, C), 1)
    wkf_t = jnp.exp(lf1 * (C - 1.0 - pos_t))
    wkb_t = jnp.exp(lb1 * pos_t)
    pos = iota((C, D), 0)
    wqf = jnp.exp(lf * (pos + 1.0))
    wqb = jnp.exp(lb * (C - pos))
    df = jnp.exp(lf * C)
    db = jnp.exp(lb * C)

    def chunk_kv(c, w_t):
        off = pl.multiple_of(c * C, C)
        kt = kt_ref[0, :, pl.ds(off, C)].astype(F32)
        return jnp.dot((kt * w_t).astype(BF16), v_ref[0, pl.ds(off, C), :], preferred_element_type=F32)

    def scan_states(t, states):
        sf, sb = states
        cb = n_chunks - 1 - t
        st_ref[t, 0:D, :] = sf.astype(BF16)
        st_ref[cb, D:2 * D, :] = sb.astype(BF16)
        return sf * df + chunk_kv(t, wkf_t), sb * db + chunk_kv(cb, wkb_t)

    zero = jnp.zeros((D, D), F32)
    lax.fori_loop(0, n_chunks, scan_states, (zero, zero), unroll=RET_UNROLL)

    def out_chunk(c, carry):
        off = pl.multiple_of(c * C, C)
        q = q_ref[0, pl.ds(off, C), :]
        v = v_ref[0, pl.ds(off, C), :]
        s = jnp.dot(q, kt_ref[0, :, pl.ds(off, C)], preferred_element_type=F32)
        o = jnp.dot((s * decay).astype(BF16), v, preferred_element_type=F32)
        qf = q.astype(F32)
        qq = jnp.concatenate([(qf * wqf).astype(BF16), (qf * wqb).astype(BF16)], axis=1)
        o = o + jnp.dot(qq, st_ref[c], preferred_element_type=F32)
        o = o * lax.rsqrt(jnp.mean(o * o, axis=-1, keepdims=True) + NORM_EPS)
        o_ref[0, pl.ds(off, C), :] = (o * g_ref[0, pl.ds(off, C), :].astype(F32)).astype(BF16)
        return carry

    lax.fori_loop(0, n_chunks, out_chunk, 0, unroll=RET_UNROLL)


def _retention(rq, rkt, rv, gate, lf, lb, B, S):
    assert S % RET_BLOCK == 0, (S, RET_BLOCK)
    n = S // RET_BLOCK
    tok = pl.BlockSpec((1, S, LANES), lambda b, h: (b, 0, h))
    tok_t = pl.BlockSpec((1, LANES, S), lambda b, h: (b, h, 0))
    dec = pl.BlockSpec((1, 1, LANES), lambda b, h: (h, 0, 0))
    return pl.pallas_call(
        functools.partial(_retention_kernel, n_chunks=n),
        grid=(B, RET_HEADS),
        in_specs=[tok, tok_t, tok, tok, dec, dec],
        out_specs=tok,
        out_shape=jax.ShapeDtypeStruct((B, S, RET_WIDTH), BF16),
        scratch_shapes=[pltpu.VMEM((n, 2 * RET_HEAD_DIM, RET_HEAD_DIM), BF16)],
        compiler_params=pltpu.CompilerParams(
            dimension_semantics=("parallel", "parallel"), vmem_limit_bytes=VMEM_LIMIT_BYTES),
        name="retention",
    )(rq.reshape(B, S, RET_WIDTH), rkt, rv.reshape(B, S, RET_WIDTH), gate.reshape(B, S, RET_WIDTH), lf, lb)


def _attention_kernel(q_ref, k_ref, vt_ref, o_ref, acc_ref, s_ref, mb_ref, *, n_q, n_kv, tq, tk):
    def produce(qi, j, slot):
        qoff = pl.multiple_of(qi * tq, tq)
        koff = pl.multiple_of(j * tk, tk)
        s = lax.dot_general(k_ref[0, pl.ds(koff, tk), :], q_ref[0, pl.ds(qoff, tq), :],
                            (((1,), (1,)), ((), ())), preferred_element_type=F32)
        s_ref[slot] = s
        mb_ref[slot] = jnp.max(s, axis=0, keepdims=True)

    def consume(j, slot, m):
        koff = pl.multiple_of(j * tk, tk)
        m_new = jnp.maximum(m, mb_ref[slot])
        alpha = jnp.exp2(m - m_new)
        p = jnp.exp2(s_ref[slot] - m_new).astype(BF16)
        pv = jnp.dot(vt_ref[0, :, pl.ds(koff, tk)], p, preferred_element_type=F32)
        acc_ref[...] = alpha * acc_ref[...] + pv
        return m_new

    def query_block(qi, carry, last_q):
        def group(g, m, last):
            j0 = g * KV_UNROLL
            for i in range(KV_UNROLL):
                if not (last and i == KV_UNROLL - 1):
                    produce(qi, j0 + i + 1, (i + 1) % 2)
                elif not last_q:
                    produce(qi + 1, 0, 0)
                m = consume(j0 + i, i % 2, m)
            return m

        n_groups = n_kv // KV_UNROLL
        acc_ref[...] = jnp.zeros_like(acc_ref)
        m = jnp.full((1, tq), -jnp.inf, F32)
        if n_groups > 1:
            m = lax.fori_loop(0, n_groups - 1, functools.partial(group, last=False), m)
        group(n_groups - 1, m, last=True)
        qoff = pl.multiple_of(qi * tq, tq)
        acc = acc_ref[...]
        o = acc[0:MLA_V_DIM, :] / acc[MLA_V_DIM:MLA_V_DIM + 1, :]
        o_ref[0, pl.ds(qoff, tq), :] = o.T.astype(BF16)
        return carry

    produce(0, 0, 0)
    if n_q > 1:
        lax.fori_loop(0, n_q - 1, functools.partial(query_block, last_q=False), 0)
    query_block(n_q - 1, 0, last_q=True)


def _attention(qcat, kcat, vt, B, S):
    tk = min(TK_ATT, S // KV_UNROLL)
    tq = SCORE_TILE // tk
    assert S % tq == 0 and S % tk == 0 and KV_UNROLL % 2 == 0 and (S // tk) % KV_UNROLL == 0, (S, tq, tk)
    qk_spec = pl.BlockSpec((1, S, MLA_QK_DIM), lambda b, h: (b, 0, h))
    return pl.pallas_call(
        functools.partial(_attention_kernel, n_q=S // tq, n_kv=S // tk, tq=tq, tk=tk),
        grid=(B, MLA_HEADS),
        in_specs=[qk_spec, qk_spec, pl.BlockSpec((1, V_EXT_ROWS, S), lambda b, h: (b, h, 0))],
        out_specs=pl.BlockSpec((1, S, MLA_V_DIM), lambda b, h: (b, 0, h)),
        out_shape=jax.ShapeDtypeStruct((B, S, MLA_WIDTH), BF16),
        scratch_shapes=[pltpu.VMEM((V_EXT_ROWS, tq), F32), pltpu.VMEM((2, tk, tq), F32),
                        pltpu.VMEM((2, 1, tq), F32)],
        compiler_params=pltpu.CompilerParams(
            dimension_semantics=("parallel", "parallel"), vmem_limit_bytes=VMEM_LIMIT_BYTES),
        name="mla_attention",
    )(qcat.reshape(B, S, MLA_HEADS * MLA_QK_DIM), kcat.reshape(B, S, MLA_HEADS * MLA_QK_DIM), vt)


def _out_proj_kernel(ret_ref, mla_ref, x_ref, w_ref, g_ref, o_ref):
    for r in range(x_ref.shape[0] // OUT_SUB_ROWS):
        rows = slice(r * OUT_SUB_ROWS, (r + 1) * OUT_SUB_ROWS)
        mix = jnp.dot(ret_ref[rows, :], w_ref[0:RET_WIDTH, :], preferred_element_type=F32)
        mix = mix + jnp.dot(mla_ref[rows, :], w_ref[RET_WIDTH:, :], preferred_element_type=F32)
        o_ref[rows, :] = x_ref[rows, :] + _rms(mix, g_ref[...])


def _out_proj(o_ret, o_mla, x2, p):
    T = x2.shape[0]
    tm = TM_OUT
    assert T % tm == 0, (T, tm)
    row = lambda i: (i, 0)
    return pl.pallas_call(
        _out_proj_kernel,
        grid=(T // tm,),
        in_specs=[
            pl.BlockSpec((tm, RET_WIDTH), row),
            pl.BlockSpec((tm, MLA_WIDTH), row),
            pl.BlockSpec((tm, D_MODEL), row),
            _const_spec((RET_WIDTH + MLA_WIDTH, D_MODEL)),
            _const_spec((1, D_MODEL)),
        ],
        out_specs=pl.BlockSpec((tm, D_MODEL), row),
        out_shape=jax.ShapeDtypeStruct((T, D_MODEL), F32),
        compiler_params=pltpu.CompilerParams(
            dimension_semantics=("parallel",), vmem_limit_bytes=VMEM_LIMIT_BYTES),
        name="out_proj",
    )(o_ret.reshape(T, RET_WIDTH), o_mla.reshape(T, MLA_WIDTH), x2, p["w_out"], p["g_mix_post"])


def _ffn_kernel(x_ref, gpre_ref, wup_ref, wdn_ref, gpost_ref, o_ref, h_ref):
    j = pl.program_id(1)

    n_chunks = x_ref.shape[0] // NORM_ROWS

    @pl.when(j == 0)
    def _():
        def pre(r, carry):
            rows = pl.ds(pl.multiple_of(r * NORM_ROWS, NORM_ROWS), NORM_ROWS)
            h_ref[rows, :] = _rms(x_ref[rows, :], gpre_ref[...]).astype(BF16)
            o_ref[rows, :] = jnp.zeros((NORM_ROWS, D_MODEL), F32)
            return carry
        lax.fori_loop(0, n_chunks, pre, 0, unroll=NORM_UNROLL)

    a = jnp.maximum(jnp.dot(h_ref[...], wup_ref[...], preferred_element_type=F32), 0.0)
    o_ref[...] += jnp.dot((a * a).astype(BF16), wdn_ref[...], preferred_element_type=F32)

    @pl.when(j == pl.num_programs(1) - 1)
    def _():
        for r in range(n_chunks):
            rows = slice(r * NORM_ROWS, (r + 1) * NORM_ROWS)
            o_ref[rows, :] = x_ref[rows, :] + _rms(o_ref[rows, :], gpost_ref[...])


def _ffn(x1, p):
    T = x1.shape[0]
    tm, tf = TM_FFN, TF_FFN
    assert T % tm == 0 and FFN_DIM % tf == 0 and tm % NORM_ROWS == 0, (T, tm, tf)
    return pl.pallas_call(
        _ffn_kernel,
        grid=(T // tm, FFN_DIM // tf),
        in_specs=[
            pl.BlockSpec((tm, D_MODEL), lambda i, j: (i, 0)),
            _const_spec((1, D_MODEL)),
            pl.BlockSpec((D_MODEL, tf), lambda i, j: (0, j)),
            pl.BlockSpec((tf, D_MODEL), lambda i, j: (j, 0)),
            _const_spec((1, D_MODEL)),
        ],
        out_specs=pl.BlockSpec((tm, D_MODEL), lambda i, j: (i, 0)),
        out_shape=jax.ShapeDtypeStruct((T, D_MODEL), F32),
        scratch_shapes=[pltpu.VMEM((tm, D_MODEL), BF16)],
        compiler_params=pltpu.CompilerParams(
            dimension_semantics=("parallel", "arbitrary"), vmem_limit_bytes=VMEM_LIMIT_BYTES),
        name="ffn",
    )(x1, p["g_ffn_pre"], p["w_ffn_up"], p["w_ffn_down"], p["g_ffn_post"])


def _rope_tables(S):
    lo_n = 64

    def cos_sin(d):
        inv = (ROPE_BASE ** (-jnp.arange(0, d, 2, dtype=F32) / d))[None, :]
        a_hi = jnp.arange(0, S, lo_n, dtype=F32)[:, None] * inv
        a_lo = jnp.arange(lo_n, dtype=F32)[:, None] * inv
        ch, sh = jnp.cos(a_hi)[:, None, :], jnp.sin(a_hi)[:, None, :]
        cl, sl = jnp.cos(a_lo)[None, :, :], jnp.sin(a_lo)[None, :, :]
        return (ch * cl - sh * sl).reshape(S, d // 2), (sh * cl + ch * sl).reshape(S, d // 2)

    c_r, s_r = cos_sin(RET_HEAD_DIM)
    c_m, s_m = cos_sin(MLA_ROPE_DIM)
    cos_r = jnp.concatenate([c_r] * 2, axis=1)
    sin_r = jnp.concatenate([-s_r, s_r], axis=1)
    cos_m = jnp.concatenate([c_m] * 4, axis=1)
    sin_m = jnp.concatenate([-s_m] * 2 + [s_m] * 2, axis=1)
    return cos_r, sin_r, cos_m, sin_m


def _layout_params(norm_mix_pre, w_in, ret_decay_fwd, ret_decay_bwd, mla_q_norm, w_q_up, mla_kv_norm, w_kv_up,
                   w_out, norm_mix_post, norm_ffn_pre, w_ffn_up, w_ffn_down, norm_ffn_post):
    half = MLA_ROPE_DIM // 2
    main = 4 * RET_WIDTH + MLA_Q_RANK + MLA_KV_RANK
    kr1, kr2 = w_in[:, main: main + half], w_in[:, main + half:]
    w_kr = jnp.concatenate([kr1, kr1, kr2, kr2], axis=1).astype(BF16)

    per_head = MLA_NOPE_DIM + MLA_ROPE_DIM
    wq = w_q_up.reshape(MLA_Q_RANK, MLA_HEADS, per_head)
    nope = wq[:, :, :MLA_NOPE_DIM].reshape(MLA_Q_RANK, MLA_HEADS * MLA_NOPE_DIM)
    x1 = wq[:, :, MLA_NOPE_DIM: MLA_NOPE_DIM + half].reshape(MLA_Q_RANK, MLA_HEADS // 2, 2 * half)
    x2 = wq[:, :, MLA_NOPE_DIM + half:].reshape(MLA_Q_RANK, MLA_HEADS // 2, 2 * half)
    slabs = jnp.concatenate([x1, x2], axis=2).reshape(MLA_Q_RANK, MLA_HEADS * MLA_ROPE_DIM)
    w_q_p = jnp.concatenate([nope, slabs], axis=1).astype(BF16)

    wkv = w_kv_up.reshape(MLA_KV_RANK, MLA_HEADS, MLA_NOPE_DIM + MLA_V_DIM)
    w_kv_p = jnp.concatenate([
        wkv[:, :, :MLA_NOPE_DIM].reshape(MLA_KV_RANK, MLA_HEADS * MLA_NOPE_DIM),
        wkv[:, :, MLA_NOPE_DIM:].reshape(MLA_KV_RANK, MLA_HEADS * MLA_V_DIM)], axis=1).astype(BF16)

    bcast = lambda v: jnp.broadcast_to(v.astype(F32)[:, None, None], (RET_HEADS, 1, LANES))
    return {
        "g_mix_pre": norm_mix_pre.reshape(1, D_MODEL), "w_in": w_in.astype(BF16), "w_kr": w_kr,
        "lf": bcast(ret_decay_fwd), "lb": bcast(ret_decay_bwd),
        "g_q": mla_q_norm.reshape(1, MLA_Q_RANK), "w_q_up": w_q_p,
        "g_kv": mla_kv_norm.reshape(1, MLA_KV_RANK), "w_kv_up": w_kv_p,
        "w_out": w_out.astype(BF16), "g_mix_post": norm_mix_post.reshape(1, D_MODEL),
        "g_ffn_pre": norm_ffn_pre.reshape(1, D_MODEL), "w_ffn_up": w_ffn_up.astype(BF16),
        "w_ffn_down": w_ffn_down.astype(BF16), "g_ffn_post": norm_ffn_post.reshape(1, D_MODEL),
    }


def _layer(x, p):
    B, S, _ = x.shape
    x2 = x.reshape(B * S, D_MODEL)
    rq, rkt, rv, gate, qcat, kcat, vt = _in_proj(x2, B, S, p)
    o_ret = _retention(rq, rkt, rv, gate, p["lf"], p["lb"], B, S)
    o_mla = _attention(qcat, kcat, vt, B, S)
    x1 = _out_proj(o_ret, o_mla, x2, p)
    return _ffn(x1, p).reshape(B, S, D_MODEL)


def kernel(x_prompt, x_sample, norm_mix_pre, w_in, ret_decay_fwd, ret_decay_bwd, mla_q_norm, w_q_up, mla_kv_norm,
           w_kv_up, w_out, norm_mix_post, norm_ffn_pre, w_ffn_up, w_ffn_down, norm_ffn_post):
    y_prompt, y_sample = x_prompt, x_sample
    tables = dict(zip(("cos_r", "sin_r", "cos_m", "sin_m"), _rope_tables(max(x_prompt.shape[1], x_sample.shape[1]))))
    for l in range(norm_mix_pre.shape[0]):
        p = _layout_params(norm_mix_pre[l], w_in[l], ret_decay_fwd[l], ret_decay_bwd[l], mla_q_norm[l], w_q_up[l],
                           mla_kv_norm[l], w_kv_up[l], w_out[l], norm_mix_post[l], norm_ffn_pre[l], w_ffn_up[l],
                           w_ffn_down[l], norm_ffn_post[l])
        p.update(tables)
        y_prompt = _layer(y_prompt, p)
        y_sample = _layer(y_sample, p)
    return (y_prompt, y_sample)
```
